```python
import math
import jax
import jax.numpy as jnp
from jax import lax
import numpy as np

D_MODEL = 4096
BATCH = 2
SEQ = 8192
DEPTH = 4

CTX_LEN = 256
GRID_W = 64
N_BRANCH = 4
MIX_W = D_MODEL // N_BRANCH
HEAD_DIM = 128
ROPE_BASE = 10000.0
Q_BLOCK = 128
NEG_INF = -1e30
EPS = 1e-6

WA_HEADS = MIX_W // HEAD_DIM
WA_KV_HEADS = 2
WA_GROUP = WA_HEADS // WA_KV_HEADS
WINDOW = 128

NA_HEADS = MIX_W // HEAD_DIM
NA_KH = 8
NA_KW = 16

MLA_HEADS = MIX_W // 128
MLA_Q_LORA = 896
MLA_KV_LORA = 512
MLA_NOPE = 128
MLA_ROPE = 64
MLA_V = 128

DIFF_DIM = 64
DIFF_HEADS = MIX_W // (2 * DIFF_DIM)
DIFF_V = 2 * DIFF_DIM

IN_SIZES = (
    WA_HEADS * HEAD_DIM, WA_KV_HEADS * HEAD_DIM, WA_KV_HEADS * HEAD_DIM, MIX_W,
    MIX_W, MIX_W, MIX_W, MIX_W,
    MLA_Q_LORA, MLA_KV_LORA, MLA_ROPE, MIX_W,
    DIFF_HEADS * 2 * DIFF_DIM, DIFF_HEADS * 2 * DIFF_DIM, DIFF_HEADS * DIFF_V, MIX_W,
    N_BRANCH * D_MODEL,
)
N_IN = sum(IN_SIZES)

kernel_name = "hybrid_parallel_branch_diffusion_trunk"


def rms_norm(x, g):
    xf = x.astype(jnp.float32)
    y = xf * lax.rsqrt(jnp.mean(xf * xf, axis=-1, keepdims=True) + EPS)
    return (y * g).astype(x.dtype)


def heads(a, *dims):
    return a.reshape(a.shape[:2] + dims)


def split_columns(p):
    points = [int(v) for v in np.cumsum(IN_SIZES)[:-1]]
    return jnp.split(p, points, axis=-1)


def axial_rope_tables(n_tok, d_rot):
    t = jnp.arange(n_tok, dtype=jnp.int32)
    row = (t // GRID_W).astype(jnp.float32)
    col = (t % GRID_W).astype(jnp.float32)
    n_f = d_rot // 4
    inv = jnp.power(ROPE_BASE, -jnp.arange(n_f, dtype=jnp.float32) / n_f)
    ang = jnp.concatenate([row[:, None] * inv, col[:, None] * inv], axis=-1)
    return jnp.cos(ang), jnp.sin(ang)


def apply_rope(x, cos, sin):
    d2 = x.shape[-1] // 2
    shape = (x.shape[1],) + (1,) * (x.ndim - 3) + (d2,)
    cs = cos.reshape(shape).astype(x.dtype)
    sn = sin.reshape(shape).astype(x.dtype)
    x1, x2 = x[..., :d2], x[..., d2:]
    return jnp.concatenate([x1 * cs - x2 * sn, x2 * cs + x1 * sn], axis=-1)


def softmax_f32(s):
    return jax.nn.softmax(s.astype(jnp.float32), axis=-1)


def sweep_query_blocks(f, *qs):
    B, S = qs[0].shape[:2]
    nb = S // Q_BLOCK
    blocks = tuple(jnp.moveaxis(a.reshape((B, nb, Q_BLOCK) + a.shape[2:]), 1, 0) for a in qs)
    out = lax.map(lambda args: f(*args), blocks)
    out = jnp.moveaxis(out, 0, 1)
    return out.reshape((B, S) + out.shape[3:])


def window_gqa(q, k, v, qc, kc, vc, qn, kn, sink, cos, sin, ctx_out):
    B, S = q.shape[:2]
    C = kc.shape[1]
    nb = S // Q_BLOCK
    G, R, d = WA_KV_HEADS, WA_GROUP, HEAD_DIM
    scale = d ** -0.5
    q = apply_rope(rms_norm(q, qn), cos, sin)
    k = apply_rope(rms_norm(k, kn), cos, sin)
    kc = rms_norm(kc, kn)
    qb = q.reshape(B, nb, Q_BLOCK, G, R, d)

    def band(a):
        ap = jnp.pad(a, ((0, 0), (Q_BLOCK, Q_BLOCK), (0, 0), (0, 0))).reshape(B, nb + 2, Q_BLOCK, G, d)
        return jnp.concatenate([ap[:, :-2], ap[:, 1:-1], ap[:, 2:]], axis=2)

    kb, vb = band(k), band(v)
    q_pos = jnp.arange(S).reshape(nb, Q_BLOCK, 1)
    k_pos = (jnp.arange(nb) * Q_BLOCK - Q_BLOCK)[:, None, None] + jnp.arange(3 * Q_BLOCK)[None, None, :]
    valid = (k_pos >= 0) & (k_pos < S) & (jnp.abs(q_pos - k_pos) <= WINDOW)
    s_lat = jnp.einsum('bnqgrd,bnkgd->bgrnqk', qb, kb).astype(jnp.float32) * scale
    s_lat = jnp.where(valid, s_lat, NEG_INF)
    s_ctx = jnp.einsum('bnqgrd,bcgd->bgrnqc', qb, kc).astype(jnp.float32) * scale
    sink_col = jnp.broadcast_to(sink.astype(jnp.float32).reshape(1, G, R, 1, 1, 1), s_ctx.shape[:-1] + (1,))
    p = softmax_f32(jnp.concatenate([s_lat, s_ctx, sink_col], axis=-1))
    nk = 3 * Q_BLOCK
    o = (jnp.einsum('bgrnqk,bnkgd->bnqgrd', p[..., :nk].astype(v.dtype), vb)
         + jnp.einsum('bgrnqc,bcgd->bnqgrd', p[..., nk:nk + C].astype(v.dtype), vc))
    o = o.reshape(B, S, WA_HEADS * d)
    oc = None
    if ctx_out:
        qcn = rms_norm(qc, qn).reshape(B, C, G, R, d)
        s = jnp.einsum('bqgrd,bkgd->bgrqk', qcn, kc).astype(jnp.float32) * scale
        sc = jnp.broadcast_to(sink.astype(jnp.float32).reshape(1, G, R, 1, 1), s.shape[:-1] + (1,))
        pc = softmax_f32(jnp.concatenate([s, sc], axis=-1))
        oc = jnp.einsum('bgrqk,bkgd->bqgrd', pc[..., :C].astype(vc.dtype), vc).reshape(B, C, WA_HEADS * d)
    return o, oc


def neighbourhood_attn(q, k, v, qc, kc, vc, qn, kn, rpb, rows, ctx_out):
    B, S = q.shape[:2]
    C = kc.shape[1]
    H, d, W = NA_HEADS, HEAD_DIM, GRID_W
    kh = min(NA_KH, rows)
    scale = d ** -0.5
    q = rms_norm(q, qn)
    k = rms_norm(k, kn)
    kc = rms_norm(kc, kn)
    qg = q.reshape(B, rows, W, H, d)
    kg = k.reshape(B, rows, W, H, d)
    vg = v.reshape(B, rows, W, H, d)
    r = jnp.arange(rows)
    row_start = jnp.clip(r - kh // 2, 0, rows - kh)
    key_rows = row_start[:, None] + jnp.arange(kh)[None, :]
    k_rows = kg[:, key_rows]
    v_rows = vg[:, key_rows]
    w = jnp.arange(W)
    col_start = jnp.clip(w - NA_KW // 2, 0, W - NA_KW)
    col_ok = (w[None, :] >= col_start[:, None]) & (w[None, :] < col_start[:, None] + NA_KW)
    dr = key_rows - r[:, None] + (NA_KH - 1)
    dc = jnp.clip(w[None, :] - w[:, None], -(NA_KW - 1), NA_KW - 1) + (NA_KW - 1)
    bias = rpb[:, dr[:, None, :, None], dc[None, :, None, :]]
    s_lat = jnp.einsum('brqhd,brkwhd->bhrqkw', qg, k_rows).astype(jnp.float32) * scale + bias.astype(jnp.float32)
    s_lat = jnp.where(col_ok[:, None, :], s_lat, NEG_INF).reshape(B, H, rows, W, kh * W)
    s_ctx = jnp.einsum('brqhd,bchd->bhrqc', qg, kc).astype(jnp.float32) * scale
    p = softmax_f32(jnp.concatenate([s_lat, s_ctx], axis=-1))
    p_lat = p[..., :kh * W].reshape(B, H, rows, W, kh, W).astype(v.dtype)
    o = (jnp.einsum('bhrqkw,brkwhd->brqhd', p_lat, v_rows)
         + jnp.einsum('bhrqc,bchd->brqhd', p[..., kh * W:].astype(v.dtype), vc))
    o = o.reshape(B, S, H * d)
    oc = None
    if ctx_out:
        qcn = rms_norm(qc, qn)
        pc = softmax_f32(jnp.einsum('bqhd,bkhd->bhqk', qcn, kc) * scale)
        oc = jnp.einsum('bhqk,bkhd->bqhd', pc.astype(vc.dtype), vc).reshape(B, C, H * d)
    return o, oc


def latent_attn(cq, ckv, kpe, cq_c, ckv_c, kpe_c, g_qa, g_kva, w_q_up, w_kv_up,
                qn_nope, qn_pe, kn_nope, kn_pe, cos, sin, ctx_out):
    B, S = cq.shape[:2]
    C = cq_c.shape[1]
    H = MLA_HEADS
    scale = (MLA_NOPE + MLA_ROPE) ** -0.5

    def queries(cqx):
        qh = heads(rms_norm(cqx, g_qa) @ w_q_up, H, MLA_NOPE + MLA_ROPE)
        return rms_norm(qh[..., :MLA_NOPE], qn_nope), rms_norm(qh[..., MLA_NOPE:], qn_pe)

    def keys_values(ckvx, kpex):
        kv = heads(rms_norm(ckvx, g_kva) @ w_kv_up, H, MLA_NOPE + MLA_V)
        return rms_norm(kv[..., :MLA_NOPE], kn_nope), rms_norm(kpex, kn_pe), kv[..., MLA_NOPE:]

    def attend(qn_, qp_, kn_, kp_, vv):
        s = (jnp.einsum('bqhd,bkhd->bhqk', qn_, kn_) + jnp.einsum('bqhd,bkd->bhqk', qp_, kp_)).astype(jnp.float32) * scale
        p = softmax_f32(s).astype(vv.dtype)
        return jnp.einsum('bhqk,bkhd->bqhd', p, vv)

    q_nope, q_pe = queries(cq)
    q_pe = apply_rope(q_pe, cos, sin)
    k_nope, k_pe, v = keys_values(ckv, kpe)
    k_pe = apply_rope(k_pe, cos, sin)
    kn_c, kpe_cn, v_c = keys_values(ckv_c, kpe_c)
    K_nope = jnp.concatenate([k_nope, kn_c], axis=1)
    K_pe = jnp.concatenate([k_pe, kpe_cn], axis=1)
    V = jnp.concatenate([v, v_c], axis=1)
    o = sweep_query_blocks(lambda a, b_: attend(a, b_, K_nope, K_pe, V), q_nope, q_pe).reshape(B, S, H * MLA_V)
    oc = None
    if ctx_out:
        qc_n, qc_p = queries(cq_c)
        oc = attend(qc_n, qc_p, kn_c, kpe_cn, v_c).reshape(B, C, H * MLA_V)
    return o, oc


def diff_attn(q, k, v, qc, kc, vc, qn, kn, lam_q1, lam_k1, lam_q2, lam_k2, subln, lambda_init, cos, sin, ctx_out):
    B, S = q.shape[:2]
    C = kc.shape[1]
    scale = DIFF_DIM ** -0.5
    lam = (jnp.exp(jnp.sum(lam_q1.astype(jnp.float32) * lam_k1.astype(jnp.float32)))
           - jnp.exp(jnp.sum(lam_q2.astype(jnp.float32) * lam_k2.astype(jnp.float32))) + lambda_init)
    q = apply_rope(rms_norm(q, qn), cos, sin)
    k = apply_rope(rms_norm(k, kn), cos, sin)
    kc = rms_norm(kc, kn)

    def attend(qq, kk, vv):
        s = jnp.einsum('bqhid,bkhid->bihqk', qq, kk).astype(jnp.float32) * scale
        p = softmax_f32(s)
        a = (p[:, 0] - lam * p[:, 1]).astype(vv.dtype)
        o = jnp.einsum('bhqk,bkhd->bqhd', a, vv)
        return rms_norm(o, subln) * (1.0 - lambda_init)

    K = jnp.concatenate([k, kc], axis=1)
    V = jnp.concatenate([v, vc], axis=1)
    o = sweep_query_blocks(lambda a: attend(a, K, V), q).reshape(B, S, DIFF_HEADS * DIFF_V)
    oc = None
    if ctx_out:
        oc = attend(rms_norm(qc, qn), kc, vc).reshape(B, C, DIFF_HEADS * DIFF_V)
    return o, oc


def merge_branches(outs, gate_paths, merge_gate, w_branch, w_out):
    mg = merge_gate.reshape(merge_gate.shape[:2] + (N_BRANCH, D_MODEL))
    merged = sum(jax.nn.sigmoid(mg[:, :, i]) * ((outs[i] * jax.nn.silu(gate_paths[i])) @ w_branch[i])
                 for i in range(N_BRANCH))
    return merged @ w_out


def setup_inputs(seed: int = 0) -> dict:
    key = jax.random.key(seed)
    ks = iter(jax.random.split(key, 40))
    L, D = DEPTH, D_MODEL

    def nrm(shape, scale):
        return jax.random.normal(next(ks), shape, jnp.float32) * scale

    def gain(shape):
        return 1.0 + nrm(shape, 0.01)

    return {
        "x": nrm((BATCH, SEQ, D), 1.0),
        "c": nrm((BATCH, D), 1.0),
        "ctx": nrm((BATCH, CTX_LEN, D), 1.0),
        "c_ctx": nrm((D,), 1.0),
        "w_ada": nrm((L, D, 3 * D), 0.5 * D ** -0.5),
        "b_ada": nrm((L, 3 * D), 0.02),
        "g_norm": gain((L, D)),
        "w_in": nrm((L, D, N_IN), D ** -0.5),
        "qn_a": gain((L, HEAD_DIM)),
        "kn_a": gain((L, HEAD_DIM)),
        "sink_a": nrm((L, WA_HEADS), 0.5),
        "qn_b": gain((L, HEAD_DIM)),
        "kn_b": gain((L, HEAD_DIM)),
        "rpb_b": nrm((L, NA_HEADS, 2 * NA_KH - 1, 2 * NA_KW - 1), 0.1),
        "g_qa": gain((L, MLA_Q_LORA)),
        "g_kva": gain((L, MLA_KV_LORA)),
        "w_q_up": nrm((L, MLA_Q_LORA, MLA_HEADS * (MLA_NOPE + MLA_ROPE)), MLA_Q_LORA ** -0.5),
        "w_kv_up": nrm((L, MLA_KV_LORA, MLA_HEADS * (MLA_NOPE + MLA_V)), MLA_KV_LORA ** -0.5),
        "qn_nope": gain((L, MLA_NOPE)),
        "qn_pe": gain((L, MLA_ROPE)),
        "kn_nope": gain((L, MLA_NOPE)),
        "kn_pe": gain((L, MLA_ROPE)),
        "qn_d": gain((L, DIFF_DIM)),
        "kn_d": gain((L, DIFF_DIM)),
        "lam_q1": nrm((L, DIFF_DIM), 0.1),
        "lam_k1": nrm((L, DIFF_DIM), 0.1),
        "lam_q2": nrm((L, DIFF_DIM), 0.1),
        "lam_k2": nrm((L, DIFF_DIM), 0.1),
        "subln_d": gain((L, DIFF_V)),
        "w_branch": nrm((L, N_BRANCH, MIX_W, D), MIX_W ** -0.5),
        "w_out": nrm((L, D, D), D ** -0.5),
    }


def reference(x, c, ctx, c_ctx, w_ada, b_ada, g_norm, w_in, qn_a, kn_a, sink_a,
              qn_b, kn_b, rpb_b, g_qa, g_kva, w_q_up, w_kv_up, qn_nope, qn_pe, kn_nope, kn_pe,
              qn_d, kn_d, lam_q1, lam_k1, lam_q2, lam_k2, subln_d, w_branch, w_out):
    B, S, _ = x.shape
    rows = S // GRID_W
    cos_a, sin_a = axial_rope_tables(S, HEAD_DIM)
    cos_c, sin_c = axial_rope_tables(S, MLA_ROPE)
    cos_d, sin_d = axial_rope_tables(S, DIFF_DIM)
    silu_c = jax.nn.silu(c)
    silu_cc = jax.nn.silu(c_ctx)
    for l in range(DEPTH):
        ctx_out = l < DEPTH - 1
        lambda_init = 0.8 - 0.6 * math.exp(-0.3 * l)
        sh, sc, gt = jnp.split(silu_c @ w_ada[l] + b_ada[l], 3, axis=-1)
        sh_c, sc_c, gt_c = jnp.split(silu_cc @ w_ada[l] + b_ada[l], 3, axis=-1)
        h = rms_norm(x, g_norm[l]) * (1.0 + sc[:, None]) + sh[:, None]
        hc = rms_norm(ctx, g_norm[l]) * (1.0 + sc_c) + sh_c
        (aq, ak, av, ag, bq, bk, bv, bg, cq, ckv, cpe, cg, dq, dk, dv, dg, mg) = split_columns(h @ w_in[l])
        (aq_c, ak_c, av_c, ag_c, bq_c, bk_c, bv_c, bg_c, cq_c, ckv_c, cpe_c, cg_c,
         dq_c, dk_c, dv_c, dg_c, mg_c) = split_columns(hc @ w_in[l])

        ya, ya_c = window_gqa(heads(aq, WA_HEADS, HEAD_DIM), heads(ak, WA_KV_HEADS, HEAD_DIM), heads(av, WA_KV_HEADS, HEAD_DIM),
                              heads(aq_c, WA_HEADS, HEAD_DIM), heads(ak_c, WA_KV_HEADS, HEAD_DIM), heads(av_c, WA_KV_HEADS, HEAD_DIM),
                              qn_a[l], kn_a[l], sink_a[l], cos_a, sin_a, ctx_out)
        yb, yb_c = neighbourhood_attn(heads(bq, NA_HEADS, HEAD_DIM), heads(bk, NA_HEADS, HEAD_DIM), heads(bv, NA_HEADS, HEAD_DIM),
                                      heads(bq_c, NA_HEADS, HEAD_DIM), heads(bk_c, NA_HEADS, HEAD_DIM), heads(bv_c, NA_HEADS, HEAD_DIM),
                                      qn_b[l], kn_b[l], rpb_b[l], rows, ctx_out)
        yc, yc_c = latent_attn(cq, ckv, cpe, cq_c, ckv_c, cpe_c, g_qa[l], g_kva[l], w_q_up[l], w_kv_up[l],
                               qn_nope[l], qn_pe[l], kn_nope[l], kn_pe[l], cos_c, sin_c, ctx_out)
        yd, yd_c = diff_attn(heads(dq, DIFF_HEADS, 2, DIFF_DIM), heads(dk, DIFF_HEADS, 2, DIFF_DIM), heads(dv, DIFF_HEADS, DIFF_V),
                             heads(dq_c, DIFF_HEADS, 2, DIFF_DIM), heads(dk_c, DIFF_HEADS, 2, DIFF_DIM), heads(dv_c, DIFF_HEADS, DIFF_V),
                             qn_d[l], kn_d[l], lam_q1[l], lam_k1[l], lam_q2[l], lam_k2[l], subln_d[l], lambda_init,
                             cos_d, sin_d, ctx_out)

        x = x + gt[:, None] * merge_branches((ya, yb, yc, yd), (ag, bg, cg, dg), mg, w_branch[l], w_out[l])
        if ctx_out:
            ctx = ctx + gt_c * merge_branches((ya_c, yb_c, yc_c, yd_c), (ag_c, bg_c, cg_c, dg_c), mg_c, w_branch[l], w_out[l])
    return x
```

```python
import functools
import math

import numpy as np
import jax
import jax.numpy as jnp
from jax import lax
from jax.experimental import pallas as pl
from jax.experimental.pallas import tpu as pltpu

GRID_W = 64
HEAD_DIM = 128
N_HEADS = 8
MIX_W = N_HEADS * HEAD_DIM
ROPE_BASE = 10000.0
NEG_INF = -1e30
EPS = 1e-6
WA_KV_HEADS = 2
WA_GROUP = N_HEADS // WA_KV_HEADS
WINDOW = 128
NA_KH = 8
NA_KW = 16
NA_QROWS = 4
NA_KROWS = NA_QROWS + NA_KH
MLA_Q_LORA = 896
MLA_KV_LORA = 512
MLA_NOPE = 128
MLA_ROPE = 64
MLA_QK = 256
DIFF_DIM = 64

LANES = 128
ROW_TILE = 256
VMEM_LIMIT = 56 * 1024 * 1024

COL_GATE = 0
COL_MG = 4096
COL_BQ, COL_BK, COL_BV = 20480, 21504, 22528
COL_DQ, COL_DK, COL_DV = 23552, 24576, 25600
COL_AQ, COL_AK, COL_AV = 26624, 27648, 27904
COL_CKV, COL_CQ, COL_CPE = 28160, 28672, 29568
N_PROJ = 29696


def _cparams(*sem):
    return pltpu.CompilerParams(dimension_semantics=sem, vmem_limit_bytes=VMEM_LIMIT)


def _dot(a, b):
    return jnp.dot(a, b, preferred_element_type=jnp.float32)


def _dot_nt(a, b):
    return lax.dot_general(a, b, (((1,), (1,)), ((), ())), preferred_element_type=jnp.float32)


def _silu(v):
    return v / (1.0 + jnp.exp(-v))


def _sigmoid(v):
    return 1.0 / (1.0 + jnp.exp(-v))


def _ada_kernel(c_ref, w_ref, b_ref, o_ref):
    s = _silu(c_ref[...]).astype(jnp.bfloat16)
    o_ref[...] = _dot(s, w_ref[...].astype(jnp.bfloat16)) + b_ref[...]


def _ada(cc, w_ada, b_ada):
    L, D, N = w_ada.shape
    tn = 512
    return pl.pallas_call(
        _ada_kernel,
        out_shape=jax.ShapeDtypeStruct((L, 8, N), jnp.float32),
        grid=(L, N // tn),
        in_specs=[
            pl.BlockSpec((8, D), lambda l, n: (0, 0)),
            pl.BlockSpec((None, D, tn), lambda l, n: (l, 0, n)),
            pl.BlockSpec((None, 1, tn), lambda l, n: (l, 0, n)),
        ],
        out_specs=pl.BlockSpec((None, 8, tn), lambda l, n: (l, 0, n)),
        compiler_params=_cparams("arbitrary", "arbitrary"),
        name="ada_mod",
    )(cc, w_ada, b_ada.reshape(L, 1, N))


def _prologue_kernel(x_ref, g_ref, sh_ref, sc_ref, o_ref):
    x = x_ref[...]
    ms = jnp.mean(x * x, axis=-1, keepdims=True)
    y = x * lax.rsqrt(ms + EPS) * g_ref[...]
    o_ref[...] = (y * (1.0 + sc_ref[...]) + sh_ref[...]).astype(o_ref.dtype)


def _mod_row(S, B):
    return lambda b, t: jnp.where(t >= S // ROW_TILE, B, b)


def _prologue(xs, g, mods3, S):
    B, T, D = xs.shape
    row = _mod_row(S, B)
    return pl.pallas_call(
        _prologue_kernel,
        out_shape=jax.ShapeDtypeStruct((B, T, D), jnp.bfloat16),
        grid=(B, T // ROW_TILE),
        in_specs=[
            pl.BlockSpec((None, ROW_TILE, D), lambda b, t: (b, t, 0)),
            pl.BlockSpec((1, D), lambda b, t: (0, 0)),
            pl.BlockSpec((None, 1, D), lambda b, t: (row(b, t), 0, 0)),
            pl.BlockSpec((None, 1, D), lambda b, t: (row(b, t), 0, 1)),
        ],
        out_specs=pl.BlockSpec((None, ROW_TILE, D), lambda b, t: (b, t, 0)),
        compiler_params=_cparams("arbitrary", "arbitrary"),
        name="norm_modulate",
    )(xs, g.reshape(1, D), mods3, mods3)


def _matmul_kernel(a_ref, w_ref, o_ref):
    o_ref[...] = _dot(a_ref[...], w_ref[...]).astype(o_ref.dtype)


def _in_proj(h, w):
    R, D = h.shape
    N = w.shape[1]
    tm, tn = 512, 1024
    return pl.pallas_call(
        _matmul_kernel,
        out_shape=jax.ShapeDtypeStruct((R, N), jnp.bfloat16),
        grid=(N // tn, R // tm),
        in_specs=[
            pl.BlockSpec((tm, D), lambda n, m: (m, 0)),
            pl.BlockSpec((D, tn), lambda n, m: (0, n)),
        ],
        out_specs=pl.BlockSpec((tm, tn), lambda n, m: (m, n)),
        compiler_params=_cparams("arbitrary", "arbitrary"),
        name="in_proj",
    )(h, w)


def _rms128(x, gain, group, count=None):
    sq = x * x
    tot = jnp.sum(sq, axis=-1, keepdims=True)
    if group == LANES:
        inv = lax.rsqrt(tot / (count or group) + EPS)
    else:
        lane = lax.broadcasted_iota(jnp.int32, x.shape, 1)
        low = lane < group
        lo = jnp.sum(jnp.where(low, sq, 0.0), axis=-1, keepdims=True)
        inv = jnp.where(low, lax.rsqrt(lo / group + EPS), lax.rsqrt((tot - lo) / group + EPS))
    return x * inv * gain


def _rope128(x, cos, sa, sb, half):
    return x * cos + pltpu.roll(x, LANES - half, 1) * sa + pltpu.roll(x, half, 1) * sb


def _normrope_kernel(q_ref, k_ref, qg_ref, kg_ref, cos_ref, sa_ref, sb_ref, qo_ref, ko_ref,
                     *, group, half, scale):
    if half:
        cos, sa, sb = cos_ref[...], sa_ref[...], sb_ref[...]
    for src, gain, dst, mul in ((q_ref, qg_ref, qo_ref, scale), (k_ref, kg_ref, ko_ref, None)):
        g = gain[...]
        for j in range(src.shape[-1] // LANES):
            sl = slice(j * LANES, (j + 1) * LANES)
            y = _rms128(src[:, sl].astype(jnp.float32), g, group)
            if half:
                y = _rope128(y, cos, sa, sb, half)
            if mul is not None:
                y = y * mul
            dst[:, sl] = y.astype(dst.dtype)


def _normrope(p, col_q, wq, col_k, wk, q_gain, k_gain, tables, *, B, T, group, half, scale, name):
    nt = T // ROW_TILE
    tab_spec = pl.BlockSpec((ROW_TILE, LANES), lambda b, t: (t, 0))
    gain_spec = pl.BlockSpec((1, LANES), lambda b, t: (0, 0))
    return pl.pallas_call(
        functools.partial(_normrope_kernel, group=group, half=half, scale=scale),
        out_shape=(jax.ShapeDtypeStruct((B, T, wq), jnp.bfloat16),
                   jax.ShapeDtypeStruct((B, T, wk), jnp.bfloat16)),
        grid=(B, nt),
        in_specs=[
            pl.BlockSpec((ROW_TILE, wq), lambda b, t: (b * nt + t, col_q // wq)),
            pl.BlockSpec((ROW_TILE, wk), lambda b, t: (b * nt + t, col_k // wk)),
            gain_spec, gain_spec, tab_spec, tab_spec, tab_spec,
        ],
        out_specs=(pl.BlockSpec((None, ROW_TILE, wq), lambda b, t: (b, t, 0)),
                   pl.BlockSpec((None, ROW_TILE, wk), lambda b, t: (b, t, 0))),
        compiler_params=_cparams("arbitrary", "arbitrary"),
        name=name,
    )(p, p, q_gain, k_gain, *tables)


def _mla_prep_kernel(cq_ref, ckv_ref, kpe_ref, gqa_ref, gkva_ref, wq_ref, wkv_ref,
                     qnn_ref, qnp_ref, knn_ref, knp_ref, cos_ref, sa_ref, sb_ref,
                     q_ref, k_ref, v_ref, *, scale):
    cos, sa, sb = cos_ref[...], sa_ref[...], sb_ref[...]
    half = MLA_ROPE // 2

    def full_rms(ref, gain_ref):
        v = ref[...].astype(jnp.float32)
        ms = jnp.mean(v * v, axis=-1, keepdims=True)
        return (v * lax.rsqrt(ms + EPS) * gain_ref[...]).astype(jnp.bfloat16)

    qh = _dot(full_rms(cq_ref, gqa_ref), wq_ref[...])
    kv = _dot(full_rms(ckv_ref, gkva_ref), wkv_ref[...])
    kpe = _rms128(kpe_ref[...].astype(jnp.float32), knp_ref[...], LANES, MLA_ROPE)
    kpe = _rope128(kpe, cos, sa, sb, half).astype(k_ref.dtype)
    qnn, qnp, knn = qnn_ref[...], qnp_ref[...], knn_ref[...]
    for h in range(N_HEADS):
        nope = slice(h * MLA_QK, h * MLA_QK + LANES)
        pe = slice(h * MLA_QK + LANES, (h + 1) * MLA_QK)
        q_ref[:, nope] = (_rms128(qh[:, nope], qnn, LANES) * scale).astype(q_ref.dtype)
        qpe = _rms128(qh[:, pe], qnp, LANES, MLA_ROPE)
        q_ref[:, pe] = (_rope128(qpe, cos, sa, sb, half) * scale).astype(q_ref.dtype)
        k_ref[:, nope] = _rms128(kv[:, h * LANES:(h + 1) * LANES], knn, LANES).astype(k_ref.dtype)
        k_ref[:, pe] = kpe
    v_ref[...] = kv[:, MIX_W:].astype(v_ref.dtype)


def _mla_prep(p, g_qa, g_kva, wq, wkv, qnn, qnp, knn, knp, tables, *, B, T, scale):
    nt = T // ROW_TILE
    row = lambda b, t: b * nt + t
    const = lambda b, t: (0, 0)
    gain_spec = pl.BlockSpec((1, LANES), const)
    tab_spec = pl.BlockSpec((ROW_TILE, LANES), lambda b, t: (t, 0))
    out_qk = jax.ShapeDtypeStruct((B, T, N_HEADS * MLA_QK), jnp.bfloat16)
    spec_qk = pl.BlockSpec((None, ROW_TILE, N_HEADS * MLA_QK), lambda b, t: (b, t, 0))
    return pl.pallas_call(
        functools.partial(_mla_prep_kernel, scale=scale),
        out_shape=(out_qk, out_qk, jax.ShapeDtypeStruct((B, T, MIX_W), jnp.bfloat16)),
        grid=(B, nt),
        in_specs=[
            pl.BlockSpec((ROW_TILE, MLA_Q_LORA), lambda b, t: (row(b, t), COL_CQ // MLA_Q_LORA)),
            pl.BlockSpec((ROW_TILE, MLA_KV_LORA), lambda b, t: (row(b, t), COL_CKV // MLA_KV_LORA)),
            pl.BlockSpec((ROW_TILE, LANES), lambda b, t: (row(b, t), COL_CPE // LANES)),
            pl.BlockSpec((1, MLA_Q_LORA), const),
            pl.BlockSpec((1, MLA_KV_LORA), const),
            pl.BlockSpec(wq.shape, const),
            pl.BlockSpec(wkv.shape, const),
            gain_spec, gain_spec, gain_spec, gain_spec, tab_spec, tab_spec, tab_spec,
        ],
        out_specs=(spec_qk, spec_qk, pl.BlockSpec((None, ROW_TILE, MIX_W), lambda b, t: (b, t, 0))),
        compiler_params=_cparams("arbitrary", "arbitrary"),
        name="mla_prep",
    )(p, p, p, g_qa, g_kva, wq, wkv, qnn, qnp, knn, knp, *tables)


def _window_kernel(q_ref, k_ref, v_ref, sink_ref, o_ref, *, S, C):
    blk = WINDOW
    band = 3 * blk
    sink = sink_ref[:, :1]
    kc = k_ref[S:S + C, :]
    vc = v_ref[S:S + C, :]

    def attend(q, parts):
        m = sink
        for s, _ in parts:
            m = jnp.maximum(m, jnp.max(s, axis=-1, keepdims=True))
        den = jnp.exp(sink - m)
        acc = None
        for s, v in parts:
            e = jnp.exp(s - m)
            den = den + jnp.sum(e, axis=-1, keepdims=True)
            pv = _dot(e.astype(v.dtype), v)
            acc = pv if acc is None else acc + pv
        return acc / den

    def body(n, carry):
        q0 = pl.multiple_of(n * blk, blk)
        q = q_ref[pl.ds(q0, blk), :]
        k0 = pl.multiple_of(jnp.clip((n - 1) * blk, 0, S - band), blk)
        kb = k_ref[pl.ds(k0, band), :]
        vb = v_ref[pl.ds(k0, band), :]
        s_lat = _dot_nt(q, kb)
        q_pos = q0 + lax.broadcasted_iota(jnp.int32, s_lat.shape, 0)
        k_pos = k0 + lax.broadcasted_iota(jnp.int32, s_lat.shape, 1)
        s_lat = jnp.where(jnp.abs(q_pos - k_pos) <= WINDOW, s_lat, NEG_INF)
        o = attend(q, [(s_lat, vb), (_dot_nt(q, kc), vc)])
        o_ref[pl.ds(q0, blk), :] = o.astype(o_ref.dtype)
        return carry

    lax.fori_loop(0, S // blk, body, 0)
    qc = q_ref[S:S + C, :]
    o_ref[S:S + C, :] = attend(qc, [(_dot_nt(qc, kc), vc)]).astype(o_ref.dtype)


def _window_attn(q, k, p, sink, *, S, C):
    B, T, _ = q.shape
    head = lambda b, h: (b, 0, h)
    return pl.pallas_call(
        functools.partial(_window_kernel, S=S, C=C),
        out_shape=jax.ShapeDtypeStruct((B, T, MIX_W), jnp.bfloat16),
        grid=(B, N_HEADS),
        in_specs=[
            pl.BlockSpec((None, T, HEAD_DIM), head),
            pl.BlockSpec((None, T, HEAD_DIM), lambda b, h: (b, 0, h // WA_GROUP)),
            pl.BlockSpec((T, HEAD_DIM), lambda b, h: (b, COL_AV // HEAD_DIM + h // WA_GROUP)),
            pl.BlockSpec((None, 1, LANES), lambda b, h: (h, 0, 0)),
        ],
        out_specs=pl.BlockSpec((None, T, HEAD_DIM), head),
        compiler_params=_cparams("arbitrary", "arbitrary"),
        name="window_attn",
    )(q, k, p, sink)


def _neighbour_kernel(q_ref, k_ref, v_ref, bias_ref, o_ref, *, S, C):
    rows = S // GRID_W
    n_blk = rows // NA_QROWS
    qn = NA_QROWS * GRID_W
    kn = NA_KROWS * GRID_W
    kc = k_ref[S:S + C, :]
    vc = v_ref[S:S + C, :]

    def softmax_pv(parts):
        m = None
        for s, _ in parts:
            mx = jnp.max(s, axis=-1, keepdims=True)
            m = mx if m is None else jnp.maximum(m, mx)
        den, acc = None, None
        for s, v in parts:
            e = jnp.exp(s - m)
            sm = jnp.sum(e, axis=-1, keepdims=True)
            pv = _dot(e.astype(v.dtype), v)
            den = sm if den is None else den + sm
            acc = pv if acc is None else acc + pv
        return acc / den

    def body(i, carry):
        q0 = pl.multiple_of(i * qn, qn)
        q = q_ref[pl.ds(q0, qn), :]
        ws = jnp.clip(i * NA_QROWS - NA_KH // 2, 0, rows - NA_KROWS)
        k0 = pl.multiple_of(ws * GRID_W, GRID_W)
        kw = k_ref[pl.ds(k0, kn), :]
        vw = v_ref[pl.ds(k0, kn), :]
        variant = jnp.where(i == 0, 0, jnp.where(i == n_blk - 1, 2, 1))
        s_lat = _dot_nt(q, kw) + bias_ref[variant]
        o = softmax_pv([(s_lat, vw), (_dot_nt(q, kc), vc)])
        o_ref[pl.ds(q0, qn), :] = o.astype(o_ref.dtype)
        return carry

    lax.fori_loop(0, n_blk, body, 0)
    qc = q_ref[S:S + C, :]
    o_ref[S:S + C, :] = softmax_pv([(_dot_nt(qc, kc), vc)]).astype(o_ref.dtype)


def _neighbour_attn(q, k, p, bias, *, S, C):
    B, T, _ = q.shape
    head = lambda b, h: (b, 0, h)
    return pl.pallas_call(
        functools.partial(_neighbour_kernel, S=S, C=C),
        out_shape=jax.ShapeDtypeStruct((B, T, MIX_W), jnp.bfloat16),
        grid=(B, N_HEADS),
        in_specs=[
            pl.BlockSpec((None, T, HEAD_DIM), head),
            pl.BlockSpec((None, T, HEAD_DIM), head),
            pl.BlockSpec((T, HEAD_DIM), lambda b, h: (b, COL_BV // HEAD_DIM + h)),
            pl.BlockSpec((3, None) + bias.shape[2:], lambda b, h: (0, h, 0, 0)),
        ],
        out_specs=pl.BlockSpec((None, T, HEAD_DIM), head),
        compiler_params=_cparams("arbitrary", "arbitrary"),
        name="neighbour_attn",
    )(q, k, p, bias)


def _neighbour_bias(rpb, rows):
    W = GRID_W
    a = np.arange(NA_QROWS)
    j = np.arange(NA_KROWS)
    w = np.arange(W)
    cs = np.clip(w - NA_KW // 2, 0, W - NA_KW)
    col_ok = (w[None, :] >= cs[:, None]) & (w[None, :] < cs[:, None] + NA_KW)
    dc = np.clip(w[None, :] - w[:, None], -(NA_KW - 1), NA_KW - 1) + (NA_KW - 1)
    out = []
    for r0 in (0, NA_QROWS, rows - NA_QROWS):
        ws = int(np.clip(r0 - NA_KH // 2, 0, rows - NA_KROWS))
        rq = r0 + a
        kr = ws + j
        rs = np.clip(rq - NA_KH // 2, 0, rows - NA_KH)
        row_ok = (kr[None, :] >= rs[:, None]) & (kr[None, :] < rs[:, None] + NA_KH)
        dr = np.clip(kr[None, :] - rq[:, None] + (NA_KH - 1), 0, 2 * NA_KH - 2)
        val = rpb[:, dr[:, None, :, None], dc[None, :, None, :]]
        ok = row_ok[:, None, :, None] & col_ok[None, :, None, :]
        val = jnp.where(ok[None], val.astype(jnp.float32), NEG_INF)
        out.append(val.reshape(rpb.shape[0], NA_QROWS * W, NA_KROWS * W))
    return jnp.stack(out)


def _flash_rows(q, k_ref, v_ref, *, S, C, chunk, is_latent):
    def scores(k0, n):
        return _dot_nt(q, k_ref[pl.ds(k0, n), :])

    s = scores(S, C)
    m = jnp.max(s, axis=-1, keepdims=True)
    e = jnp.exp(s - m)
    l = jnp.sum(e, axis=-1, keepdims=True)
    acc = _dot(e.astype(jnp.bfloat16), v_ref[pl.ds(S, C), :])

    def body(j, carry):
        m, l, acc = carry
        k0 = pl.multiple_of(j * chunk, chunk)
        s = scores(k0, chunk)
        m_new = jnp.maximum(m, jnp.max(s, axis=-1, keepdims=True))
        alpha = jnp.exp(m - m_new)
        e = jnp.exp(s - m_new)
        l = alpha * l + jnp.sum(e, axis=-1, keepdims=True)
        acc = alpha * acc + _dot(e.astype(jnp.bfloat16), v_ref[pl.ds(k0, chunk), :])
        return m_new, l, acc

    n_chunks = jnp.where(is_latent, S // chunk, 0)
    m, l, acc = lax.fori_loop(0, n_chunks, body, (m, l, acc))
    return acc / l


def _mla_kernel(q_ref, k_ref, v_ref, o_ref, *, S, C, chunk):
    is_latent = pl.program_id(2) < S // ROW_TILE
    o = _flash_rows(q_ref[...], k_ref, v_ref, S=S, C=C, chunk=chunk, is_latent=is_latent)
    o_ref[...] = o.astype(o_ref.dtype)


def _mla_attn(q, k, v, *, S, C):
    B, T, _ = q.shape
    return pl.pallas_call(
        functools.partial(_mla_kernel, S=S, C=C, chunk=512),
        out_shape=jax.ShapeDtypeStruct((B, T, MIX_W), jnp.bfloat16),
        grid=(B, N_HEADS, T // ROW_TILE),
        in_specs=[
            pl.BlockSpec((None, ROW_TILE, MLA_QK), lambda b, h, t: (b, t, h)),
            pl.BlockSpec((None, T, MLA_QK), lambda b, h, t: (b, 0, h)),
            pl.BlockSpec((None, T, HEAD_DIM), lambda b, h, t: (b, 0, h)),
        ],
        out_specs=pl.BlockSpec((None, ROW_TILE, HEAD_DIM), lambda b, h, t: (b, t, h)),
        compiler_params=_cparams("arbitrary", "arbitrary", "arbitrary"),
        name="mla_attn",
    )(q, k, v)


def _diff_kernel(q_ref, k_ref, v_ref, lam_ref, sub_ref, o_ref, *, S, C, chunk, lambda_init):
    is_latent = pl.program_id(2) < S // ROW_TILE
    q = q_ref[...]
    low = lax.broadcasted_iota(jnp.int32, q.shape, 1) < DIFF_DIM
    zero = jnp.zeros_like(q)
    q2 = jnp.concatenate([jnp.where(low, q, zero), jnp.where(low, zero, q)], axis=0)
    o = _flash_rows(q2, k_ref, v_ref, S=S, C=C, chunk=chunk, is_latent=is_latent)
    lv = lam_ref[...]
    lam = (jnp.exp(jnp.sum(lv[0:1] * lv[1:2], axis=-1, keepdims=True))
           - jnp.exp(jnp.sum(lv[2:3] * lv[3:4], axis=-1, keepdims=True)) + lambda_init)
    d = o[:ROW_TILE] - lam * o[ROW_TILE:]
    ms = jnp.mean(d * d, axis=-1, keepdims=True)
    o_ref[...] = (d * lax.rsqrt(ms + EPS) * sub_ref[...] * (1.0 - lambda_init)).astype(o_ref.dtype)


def _diff_attn(q, k, p, lam_rows, subln, *, S, C, lambda_init):
    B, T, _ = q.shape
    const = lambda b, h, t: (0, 0)
    return pl.pallas_call(
        functools.partial(_diff_kernel, S=S, C=C, chunk=512, lambda_init=lambda_init),
        out_shape=jax.ShapeDtypeStruct((B, T, MIX_W), jnp.bfloat16),
        grid=(B, N_HEADS, T // ROW_TILE),
        in_specs=[
            pl.BlockSpec((None, ROW_TILE, HEAD_DIM), lambda b, h, t: (b, t, h)),
            pl.BlockSpec((None, T, HEAD_DIM), lambda b, h, t: (b, 0, h)),
            pl.BlockSpec((T, HEAD_DIM), lambda b, h, t: (b, COL_DV // HEAD_DIM + h)),
            pl.BlockSpec((8, LANES), const),
            pl.BlockSpec((1, LANES), const),
        ],
        out_specs=pl.BlockSpec((None, ROW_TILE, HEAD_DIM), lambda b, h, t: (b, t, h)),
        compiler_params=_cparams("arbitrary", "arbitrary", "arbitrary"),
        name="diff_attn",
    )(q, k, p, lam_rows, subln)


def _merge_kernel(ya_ref, yb_ref, yc_ref, yd_ref, g_ref, m0_ref, m1_ref, m2_ref, m3_ref, w_ref,
                  o_ref, u_ref):
    ys = (ya_ref, yb_ref, yc_ref, yd_ref)
    mgs = (m0_ref, m1_ref, m2_ref, m3_ref)

    @pl.when(pl.program_id(1) == 0)
    def _():
        for i in range(4):
            g = g_ref[:, i * MIX_W:(i + 1) * MIX_W].astype(jnp.float32)
            u_ref[i] = (ys[i][...].astype(jnp.float32) * _silu(g)).astype(u_ref.dtype)

    acc = None
    for i in range(4):
        t = _sigmoid(mgs[i][...].astype(jnp.float32)) * _dot(u_ref[i], w_ref[i])
        acc = t if acc is None else acc + t
    o_ref[...] = acc.astype(o_ref.dtype)


def _merge(ys, p, wb):
    R = p.shape[0]
    D = wb.shape[-1]
    tm, tn = 512, 1024
    y_spec = pl.BlockSpec((tm, MIX_W), lambda m, n: (m, 0))
    mg_specs = [pl.BlockSpec((tm, tn), functools.partial(lambda m, n, i: (m, (COL_MG + i * D) // tn + n), i=i))
                for i in range(4)]
    return pl.pallas_call(
        _merge_kernel,
        out_shape=jax.ShapeDtypeStruct((R, D), jnp.bfloat16),
        grid=(R // tm, D // tn),
        in_specs=[y_spec, y_spec, y_spec, y_spec,
                  pl.BlockSpec((tm, 4 * MIX_W), lambda m, n: (m, COL_GATE // (4 * MIX_W))),
                  *mg_specs,
                  pl.BlockSpec((4, MIX_W, tn), lambda m, n: (0, 0, n))],
        out_specs=pl.BlockSpec((tm, tn), lambda m, n: (m, n)),
        scratch_shapes=[pltpu.VMEM((4, tm, MIX_W), jnp.bfloat16)],
        compiler_params=_cparams("arbitrary", "arbitrary"),
        name="merge",
    )(*[y.reshape(R, MIX_W) for y in ys], p, p, p, p, p, wb)


def _out_kernel(a_ref, w_ref, x_ref, gt_ref, o_ref):
    o_ref[...] = x_ref[...] + gt_ref[...] * _dot(a_ref[...], w_ref[...])


def _out_proj(merged, w, xs, mods3, S):
    B, T, D = xs.shape
    tn = 1024
    row = _mod_row(S, B)
    return pl.pallas_call(
        _out_kernel,
        out_shape=jax.ShapeDtypeStruct((B, T, D), jnp.float32),
        grid=(D // tn, B, T // ROW_TILE),
        in_specs=[
            pl.BlockSpec((None, ROW_TILE, D), lambda n, b, t: (b, t, 0)),
            pl.BlockSpec((D, tn), lambda n, b, t: (0, n)),
            pl.BlockSpec((None, ROW_TILE, tn), lambda n, b, t: (b, t, n)),
            pl.BlockSpec((None, 1, tn), lambda n, b, t: (row(b, t), 0, 2 * (D // tn) + n)),
        ],
        out_specs=pl.BlockSpec((None, ROW_TILE, tn), lambda n, b, t: (b, t, n)),
        compiler_params=_cparams("arbitrary", "arbitrary", "arbitrary"),
        name="out_proj",
    )(merged.reshape(B, T, D), w, xs, mods3)


def _rope_tables(S, C, d_rot):
    t = jnp.arange(S, dtype=jnp.int32)
    row = (t // GRID_W).astype(jnp.float32)
    col = (t % GRID_W).astype(jnp.float32)
    n_f = d_rot // 4
    inv = jnp.power(ROPE_BASE, -jnp.arange(n_f, dtype=jnp.float32) / n_f)
    ang = jnp.concatenate([row[:, None] * inv, col[:, None] * inv], axis=-1)
    cos, sin = jnp.cos(ang), jnp.sin(ang)
    zero = jnp.zeros_like(sin)
    reps = LANES // d_rot
    cos_t = jnp.tile(jnp.concatenate([cos, cos], axis=-1), (1, reps))
    sa_t = jnp.tile(jnp.concatenate([-sin, zero], axis=-1), (1, reps))
    sb_t = jnp.tile(jnp.concatenate([zero, sin], axis=-1), (1, reps))
    pad = lambda a, v: jnp.concatenate([a, jnp.full((C, LANES), v, jnp.float32)], axis=0)
    return pad(cos_t, 1.0), pad(sa_t, 0.0), pad(sb_t, 0.0)


def _split_w_in(w):
    sizes = (1024, 256, 256, 1024, 1024, 1024, 1024, 1024, MLA_Q_LORA, MLA_KV_LORA, MLA_ROPE, 1024,
             1024, 1024, 1024, 1024, 4 * w.shape[1])
    pts = [int(v) for v in np.cumsum(sizes)[:-1]]
    (aq, ak, av, ag, bq, bk, bv, bg, cq, ckv, cpe, cg, dq, dk, dv, dg, mg) = jnp.split(w, pts, axis=-1)
    pad = jnp.zeros(w.shape[:2] + (LANES - MLA_ROPE,), w.dtype)
    out = jnp.concatenate([ag, bg, cg, dg, mg, bq, bk, bv, dq, dk, dv, aq, ak, av, ckv, cq, cpe, pad], axis=-1)
    assert out.shape[-1] == N_PROJ
    return out.astype(jnp.bfloat16)


def _split_w_q_up(w):
    L, K, _ = w.shape
    w = w.reshape(L, K, N_HEADS, MLA_NOPE + MLA_ROPE)
    w = jnp.pad(w, ((0, 0), (0, 0), (0, 0), (0, MLA_QK - MLA_NOPE - MLA_ROPE)))
    return w.reshape(L, K, N_HEADS * MLA_QK).astype(jnp.bfloat16)


def _split_w_kv_up(w):
    L, K, _ = w.shape
    w = w.reshape(L, K, N_HEADS, 2, HEAD_DIM)
    return jnp.swapaxes(w, 2, 3).reshape(L, K, 2 * MIX_W).astype(jnp.bfloat16)


def _lane_row(v, reps=1, pad=0):
    r = jnp.tile(v.astype(jnp.float32), reps)
    return jnp.pad(r, (0, pad)).reshape(1, LANES)


def kernel(x, c, ctx, c_ctx, w_ada, b_ada, g_norm, w_in, qn_a, kn_a, sink_a, qn_b, kn_b, rpb_b, g_qa, g_kva,
           w_q_up, w_kv_up, qn_nope, qn_pe, kn_nope, kn_pe, qn_d, kn_d, lam_q1, lam_k1, lam_q2, lam_k2,
           subln_d, w_branch, w_out):
    B, S, D = x.shape
    C = ctx.shape[1]
    L = w_ada.shape[0]
    T = S + C
    R = B * T
    rows = S // GRID_W
    assert D == 4 * MIX_W and S % (4 * ROW_TILE) == 0 and C == ROW_TILE and B + 1 <= 8
    assert rows % NA_QROWS == 0 and rows >= NA_KROWS + NA_QROWS

    xs = jnp.concatenate([x, ctx], axis=1)
    cc = jnp.zeros((8, D), jnp.float32).at[:B].set(c).at[B].set(c_ctx)
    mods = _ada(cc, w_ada, b_ada)

    w_in_p = _split_w_in(w_in)
    w_q_p = _split_w_q_up(w_q_up)
    w_kv_p = _split_w_kv_up(w_kv_up)
    w_branch_p = w_branch.astype(jnp.bfloat16)
    w_out_p = w_out.astype(jnp.bfloat16)
    tab_a = _rope_tables(S, C, HEAD_DIM)
    tab_d = _rope_tables(S, C, DIFF_DIM)

    for l in range(L):
        lambda_init = 0.8 - 0.6 * math.exp(-0.3 * l)
        mods3 = mods[l].reshape(8, 1, 3 * D)
        h = _prologue(xs, g_norm[l], mods3, S)
        p = _in_proj(h.reshape(R, D), w_in_p[l])

        qa, ka = _normrope(p, COL_AQ, MIX_W, COL_AK, WA_KV_HEADS * HEAD_DIM, _lane_row(qn_a[l]), _lane_row(kn_a[l]),
                           tab_a, B=B, T=T, group=HEAD_DIM, half=HEAD_DIM // 2, scale=HEAD_DIM ** -0.5,
                           name="prep_window")
        sink = jnp.broadcast_to(sink_a[l].astype(jnp.float32)[:, None, None], (N_HEADS, 1, LANES))
        ya = _window_attn(qa, ka, p, sink, S=S, C=C)

        qb, kb = _normrope(p, COL_BQ, MIX_W, COL_BK, MIX_W, _lane_row(qn_b[l]), _lane_row(kn_b[l]),
                           tab_a, B=B, T=T, group=HEAD_DIM, half=0, scale=HEAD_DIM ** -0.5,
                           name="prep_neighbour")
        yb = _neighbour_attn(qb, kb, p, _neighbour_bias(rpb_b[l], rows), S=S, C=C)

        qc, kc, vc = _mla_prep(p, g_qa[l].reshape(1, -1), g_kva[l].reshape(1, -1), w_q_p[l], w_kv_p[l],
                               _lane_row(qn_nope[l]), _lane_row(qn_pe[l], pad=LANES - MLA_ROPE),
                               _lane_row(kn_nope[l]), _lane_row(kn_pe[l], pad=LANES - MLA_ROPE),
                               tab_d, B=B, T=T, scale=(MLA_NOPE + MLA_ROPE) ** -0.5)
        yc = _mla_attn(qc, kc, vc, S=S, C=C)

        qd, kd = _normrope(p, COL_DQ, MIX_W, COL_DK, MIX_W, _lane_row(qn_d[l], reps=2), _lane_row(kn_d[l], reps=2),
                           tab_d, B=B, T=T, group=DIFF_DIM, half=DIFF_DIM // 2, scale=DIFF_DIM ** -0.5,
                           name="prep_diff")
        lam_rows = jnp.zeros((8, LANES), jnp.float32).at[:4, :DIFF_DIM].set(
            jnp.stack([lam_q1[l], lam_k1[l], lam_q2[l], lam_k2[l]]).astype(jnp.float32))
        yd = _diff_attn(qd, kd, p, lam_rows, _lane_row(subln_d[l]), S=S, C=C, lambda_init=lambda_init)

        merged = _merge((ya, yb, yc, yd), p, w_branch_p[l])
        xs = _out_proj(merged, w_out_p[l], xs, mods3, S)
    return xs[:, :S]
```

```python
import functools
import math

import numpy as np
import jax
import jax.numpy as jnp
from jax import lax
from jax.experimental import pallas as pl
from jax.experimental.pallas import tpu as pltpu

GRID_W = 64
HEAD_DIM = 128
N_HEADS = 8
MIX_W = N_HEADS * HEAD_DIM
ROPE_BASE = 10000.0
NEG_INF = -1e30
EPS = 1e-6
LOG2E = math.log2(math.e)
WA_KV_HEADS = 2
WA_GROUP = N_HEADS // WA_KV_HEADS
WINDOW = 128
NA_KH = 8
NA_KW = 16
NA_QROWS = 4
NA_KROWS = NA_QROWS + NA_KH
MLA_Q_LORA = 896
MLA_KV_LORA = 512
MLA_NOPE = 128
MLA_ROPE = 64
MLA_QK = 256
DIFF_DIM = 64

LANES = 128
BF16_ROWS = 16
ROW_TILE = 256
VMEM_LIMIT = 56 * 1024 * 1024

P1_W = 8192
P2_START = 8128
C1_AQ, C1_AK, C1_AV, C1_AG = 0, 1024, 1280, 1536
C1_BQ, C1_BK, C1_BV, C1_BG = 2560, 3584, 4608, 5632
C1_MLA, MLA_IN_W = 6144, 2048
X_CQ, X_CKV, X_KPE = 512, 1408, 1920
C2_CG, C2_DQ, C2_DK, C2_DV, C2_DG, C2_MG = 0, 1024, 2048, 3072, 4096, 5120


def _cparams(*sem):
    return pltpu.CompilerParams(dimension_semantics=sem, vmem_limit_bytes=VMEM_LIMIT)


def _dot(a, b):
    return jnp.dot(a, b, preferred_element_type=jnp.float32)


def _dot_nt(a, b):
    return lax.dot_general(a, b, (((1,), (1,)), ((), ())), preferred_element_type=jnp.float32)


def _silu(v):
    return v / (1.0 + jnp.exp(-v))


def _sigmoid(v):
    return 1.0 / (1.0 + jnp.exp(-v))


def _row_tile(n, cap):
    return max(t for t in range(BF16_ROWS, cap + 1, BF16_ROWS) if n % t == 0)


def _ada_kernel(c_ref, w_ref, b_ref, o_ref):
    s = _silu(c_ref[...]).astype(jnp.bfloat16)
    o_ref[...] = _dot(s, w_ref[...].astype(jnp.bfloat16)) + b_ref[...]


def _ada(cc, w_ada, b_ada):
    L, D, N = w_ada.shape
    tn = 512
    return pl.pallas_call(
        _ada_kernel,
        out_shape=jax.ShapeDtypeStruct((L, 8, N), jnp.float32),
        grid=(L, N // tn),
        in_specs=[
            pl.BlockSpec((8, D), lambda l, n: (0, 0)),
            pl.BlockSpec((None, D, tn), lambda l, n: (l, 0, n)),
            pl.BlockSpec((None, 1, tn), lambda l, n: (l, 0, n)),
        ],
        out_specs=pl.BlockSpec((None, 8, tn), lambda l, n: (l, 0, n)),
        compiler_params=_cparams("arbitrary", "arbitrary"),
        name="ada_mod",
    )(cc, w_ada, b_ada.reshape(L, 1, N))


def _prologue_kernel(x_ref, g_ref, sh_ref, sc_ref, o_ref):
    x = x_ref[...]
    ms = jnp.mean(x * x, axis=-1, keepdims=True)
    y = x * lax.rsqrt(ms + EPS) * g_ref[...]
    o_ref[...] = (y * (1.0 + sc_ref[...]) + sh_ref[...]).astype(o_ref.dtype)


def _mod_row(S, B):
    return lambda b, t: jnp.where(t >= S // ROW_TILE, B, b)


def _prologue(xs, g, mods4, l, S):
    B, T, D = xs.shape
    row = _mod_row(S, B)
    return pl.pallas_call(
        _prologue_kernel,
        out_shape=jax.ShapeDtypeStruct((B, T, D), jnp.bfloat16),
        grid=(B, T // ROW_TILE),
        in_specs=[
            pl.BlockSpec((None, ROW_TILE, D), lambda b, t: (b, t, 0)),
            pl.BlockSpec((1, D), lambda b, t: (0, 0)),
            pl.BlockSpec((None, None, 1, D), lambda b, t: (l, row(b, t), 0, 0)),
            pl.BlockSpec((None, None, 1, D), lambda b, t: (l, row(b, t), 0, 1)),
        ],
        out_specs=pl.BlockSpec((None, ROW_TILE, D), lambda b, t: (b, t, 0)),
        compiler_params=_cparams("arbitrary", "arbitrary"),
        name="norm_modulate",
    )(xs, g.reshape(1, D), mods4, mods4)


def _matmul_kernel(a_ref, w_ref, o_ref):
    o_ref[...] = _dot(a_ref[...], w_ref[...]).astype(o_ref.dtype)


def _in_proj(h, w, l, name):
    R, D = h.shape
    N = w.shape[2]
    tm, tn = _row_tile(R, 1100), 1024
    return pl.pallas_call(
        _matmul_kernel,
        out_shape=jax.ShapeDtypeStruct((R, N), jnp.bfloat16),
        grid=(N // tn, R // tm),
        in_specs=[
            pl.BlockSpec((tm, D), lambda n, m: (m, 0)),
            pl.BlockSpec((None, D, tn), lambda n, m: (l, 0, n)),
        ],
        out_specs=pl.BlockSpec((tm, tn), lambda n, m: (m, n)),
        compiler_params=_cparams("arbitrary", "arbitrary"),
        name=name,
    )(h, w)


def _rms128(x, gain, group, count=None):
    sq = x * x
    tot = jnp.sum(sq, axis=-1, keepdims=True)
    if group == LANES:
        inv = lax.rsqrt(tot / (count or group) + EPS)
    else:
        lane = lax.broadcasted_iota(jnp.int32, x.shape, 1)
        low = lane < group
        lo = jnp.sum(jnp.where(low, sq, 0.0), axis=-1, keepdims=True)
        inv = jnp.where(low, lax.rsqrt(lo / group + EPS), lax.rsqrt((tot - lo) / group + EPS))
    return x * inv * gain


def _rope128(x, cos, sa, sb, half):
    return x * cos + pltpu.roll(x, LANES - half, 1) * sa + pltpu.roll(x, half, 1) * sb


def _normrope_kernel(q_ref, k_ref, qg_ref, kg_ref, cos_ref, sa_ref, sb_ref, qo_ref, ko_ref,
                     *, group, half, scale):
    cos, sa, sb = cos_ref[...], sa_ref[...], sb_ref[...]
    for src, gain, dst, mul in ((q_ref, qg_ref, qo_ref, scale), (k_ref, kg_ref, ko_ref, None)):
        g = gain[...]
        for j in range(src.shape[-1] // LANES):
            sl = slice(j * LANES, (j + 1) * LANES)
            y = _rope128(_rms128(src[:, sl].astype(jnp.float32), g, group), cos, sa, sb, half)
            if mul is not None:
                y = y * mul
            dst[:, sl] = y.astype(dst.dtype)


def _normrope(p, col_q, wq, col_k, wk, q_gain, k_gain, tables, *, B, T, group, half, scale, name):
    nt = T // ROW_TILE
    tab_spec = pl.BlockSpec((ROW_TILE, LANES), lambda b, t: (t, 0))
    gain_spec = pl.BlockSpec((1, LANES), lambda b, t: (0, 0))
    return pl.pallas_call(
        functools.partial(_normrope_kernel, group=group, half=half, scale=scale),
        out_shape=(jax.ShapeDtypeStruct((B, T, wq), jnp.bfloat16),
                   jax.ShapeDtypeStruct((B, T, wk), jnp.bfloat16)),
        grid=(B, nt),
        in_specs=[
            pl.BlockSpec((ROW_TILE, wq), lambda b, t: (b * nt + t, col_q // wq)),
            pl.BlockSpec((ROW_TILE, wk), lambda b, t: (b * nt + t, col_k // wk)),
            gain_spec, gain_spec, tab_spec, tab_spec, tab_spec,
        ],
        out_specs=(pl.BlockSpec((None, ROW_TILE, wq), lambda b, t: (b, t, 0)),
                   pl.BlockSpec((None, ROW_TILE, wk), lambda b, t: (b, t, 0))),
        compiler_params=_cparams("arbitrary", "arbitrary"),
        name=name,
    )(p, p, q_gain, k_gain, *tables)


def _mla_prep_kernel(x_ref, gqa_ref, gkva_ref, wq_ref, wkv_ref,
                     qnn_ref, qnp_ref, knn_ref, knp_ref, cos_ref, sa_ref, sb_ref,
                     q_ref, k_ref, v_ref, *, scale):
    cos, sa, sb = cos_ref[...], sa_ref[...], sb_ref[...]
    half = MLA_ROPE // 2

    def full_rms(v, gain_ref):
        v = v.astype(jnp.float32)
        ms = jnp.mean(v * v, axis=-1, keepdims=True)
        return (v * lax.rsqrt(ms + EPS) * gain_ref[...]).astype(jnp.bfloat16)

    cq = x_ref[:, X_CQ:X_CQ + MLA_Q_LORA]
    ckv = x_ref[:, X_CKV:X_CKV + MLA_KV_LORA]
    kpe = x_ref[:, X_KPE:X_KPE + LANES].astype(jnp.float32)
    kpe = jnp.where(lax.broadcasted_iota(jnp.int32, kpe.shape, 1) < MLA_ROPE, kpe, 0.0)
    qh = _dot(full_rms(cq, gqa_ref), wq_ref[...])
    kv = _dot(full_rms(ckv, gkva_ref), wkv_ref[...])
    kpe = _rope128(_rms128(kpe, knp_ref[...], LANES, MLA_ROPE), cos, sa, sb, half).astype(k_ref.dtype)
    qnn, qnp, knn = qnn_ref[...], qnp_ref[...], knn_ref[...]
    for h in range(N_HEADS):
        nope = slice(h * MLA_QK, h * MLA_QK + LANES)
        pe = slice(h * MLA_QK + LANES, (h + 1) * MLA_QK)
        q_ref[:, nope] = (_rms128(qh[:, nope], qnn, LANES) * scale).astype(q_ref.dtype)
        qpe = _rms128(qh[:, pe], qnp, LANES, MLA_ROPE)
        q_ref[:, pe] = (_rope128(qpe, cos, sa, sb, half) * scale).astype(q_ref.dtype)
        k_ref[:, nope] = _rms128(kv[:, h * LANES:(h + 1) * LANES], knn, LANES).astype(k_ref.dtype)
        k_ref[:, pe] = kpe
    v_ref[...] = kv[:, MIX_W:].astype(v_ref.dtype)


def _mla_prep(p1, g_qa, g_kva, wq, wkv, l, qnn, qnp, knn, knp, tables, *, B, T, scale):
    nt = T // ROW_TILE
    const = lambda b, t: (0, 0)
    gain_spec = pl.BlockSpec((1, LANES), const)
    tab_spec = pl.BlockSpec((ROW_TILE, LANES), lambda b, t: (t, 0))
    out_qk = jax.ShapeDtypeStruct((B, T, N_HEADS * MLA_QK), jnp.bfloat16)
    spec_qk = pl.BlockSpec((None, ROW_TILE, N_HEADS * MLA_QK), lambda b, t: (b, t, 0))
    return pl.pallas_call(
        functools.partial(_mla_prep_kernel, scale=scale),
        out_shape=(out_qk, out_qk, jax.ShapeDtypeStruct((B, T, MIX_W), jnp.bfloat16)),
        grid=(B, nt),
        in_specs=[
            pl.BlockSpec((ROW_TILE, MLA_IN_W), lambda b, t: (b * nt + t, C1_MLA // MLA_IN_W)),
            pl.BlockSpec((1, MLA_Q_LORA), const),
            pl.BlockSpec((1, MLA_KV_LORA), const),
            pl.BlockSpec((None,) + wq.shape[1:], lambda b, t: (l, 0, 0)),
            pl.BlockSpec((None,) + wkv.shape[1:], lambda b, t: (l, 0, 0)),
            gain_spec, gain_spec, gain_spec, gain_spec, tab_spec, tab_spec, tab_spec,
        ],
        out_specs=(spec_qk, spec_qk, pl.BlockSpec((None, ROW_TILE, MIX_W), lambda b, t: (b, t, 0))),
        compiler_params=_cparams("arbitrary", "arbitrary"),
        name="mla_prep",
    )(p1, g_qa, g_kva, wq, wkv, qnn, qnp, knn, knp, *tables)


def _window_kernel(q_ref, k_ref, v_ref, g_ref, sink_ref, o_ref, *, S, C):
    blk = WINDOW
    band = 3 * blk
    sink = sink_ref[:, :1]
    kc = k_ref[S:S + C, :]
    vc = v_ref[S:S + C, :]

    def attend(parts):
        m = sink
        for s, _ in parts:
            m = jnp.maximum(m, jnp.max(s, axis=-1, keepdims=True))
        den = jnp.exp(sink - m)
        acc = None
        for s, v in parts:
            e = jnp.exp(s - m)
            den = den + jnp.sum(e, axis=-1, keepdims=True)
            pv = _dot(e.astype(v.dtype), v)
            acc = pv if acc is None else acc + pv
        return acc / den

    def store(r0, n, o):
        g = g_ref[pl.ds(r0, n), :].astype(jnp.float32)
        o_ref[pl.ds(r0, n), :] = (o * _silu(g)).astype(o_ref.dtype)

    def body(n, carry):
        q0 = pl.multiple_of(n * blk, blk)
        q = q_ref[pl.ds(q0, blk), :]
        k0 = pl.multiple_of(jnp.clip((n - 1) * blk, 0, S - band), blk)
        kb = k_ref[pl.ds(k0, band), :]
        vb = v_ref[pl.ds(k0, band), :]
        s_lat = _dot_nt(q, kb)
        q_pos = q0 + lax.broadcasted_iota(jnp.int32, s_lat.shape, 0)
        k_pos = k0 + lax.broadcasted_iota(jnp.int32, s_lat.shape, 1)
        s_lat = jnp.where(jnp.abs(q_pos - k_pos) <= WINDOW, s_lat, NEG_INF)
        store(q0, blk, attend([(s_lat, vb), (_dot_nt(q, kc), vc)]))
        return carry

    lax.fori_loop(0, S // blk, body, 0, unroll=2)
    qc = q_ref[S:S + C, :]
    store(S, C, attend([(_dot_nt(qc, kc), vc)]))


def _window_attn(q, k, p1, sink, *, S, C):
    B, T, _ = q.shape
    head = lambda b, h: (b, 0, h)
    return pl.pallas_call(
        functools.partial(_window_kernel, S=S, C=C),
        out_shape=jax.ShapeDtypeStruct((B, T, MIX_W), jnp.bfloat16),
        grid=(B, N_HEADS),
        in_specs=[
            pl.BlockSpec((None, T, HEAD_DIM), head),
            pl.BlockSpec((None, T, HEAD_DIM), lambda b, h: (b, 0, h // WA_GROUP)),
            pl.BlockSpec((T, HEAD_DIM), lambda b, h: (b, C1_AV // HEAD_DIM + h // WA_GROUP)),
            pl.BlockSpec((T, HEAD_DIM), lambda b, h: (b, C1_AG // HEAD_DIM + h)),
            pl.BlockSpec((None, 1, LANES), lambda b, h: (h, 0, 0)),
        ],
        out_specs=pl.BlockSpec((None, T, HEAD_DIM), head),
        compiler_params=_cparams("arbitrary", "arbitrary"),
        name="window_attn",
    )(q, k, p1, p1, sink)


def _neighbour_kernel(q_ref, k_ref, v_ref, g_ref, qg_ref, kg_ref, bias_ref, o_ref, qn_ref, kn_ref,
                      *, S, C, scale):
    rows = S // GRID_W
    n_blk = rows // NA_QROWS
    qn = NA_QROWS * GRID_W
    kn = NA_KROWS * GRID_W
    T = S + C

    def norm_body(i, carry):
        r0 = pl.multiple_of(i * ROW_TILE, ROW_TILE)
        q = _rms128(q_ref[pl.ds(r0, ROW_TILE), :].astype(jnp.float32), qg_ref[...], LANES) * scale
        qn_ref[pl.ds(r0, ROW_TILE), :] = q.astype(qn_ref.dtype)
        k = _rms128(k_ref[pl.ds(r0, ROW_TILE), :].astype(jnp.float32), kg_ref[...], LANES)
        kn_ref[pl.ds(r0, ROW_TILE), :] = k.astype(kn_ref.dtype)
        return carry

    lax.fori_loop(0, T // ROW_TILE, norm_body, 0)
    kc = kn_ref[S:S + C, :]
    vc = v_ref[S:S + C, :]

    def softmax_pv(parts):
        m = None
        for s, _ in parts:
            mx = jnp.max(s, axis=-1, keepdims=True)
            m = mx if m is None else jnp.maximum(m, mx)
        den, acc = None, None
        for s, v in parts:
            e = jnp.exp(s - m)
            sm = jnp.sum(e, axis=-1, keepdims=True)
            pv = _dot(e.astype(v.dtype), v)
            den = sm if den is None else den + sm
            acc = pv if acc is None else acc + pv
        return acc / den

    def store(r0, n, o):
        g = g_ref[pl.ds(r0, n), :].astype(jnp.float32)
        o_ref[pl.ds(r0, n), :] = (o * _silu(g)).astype(o_ref.dtype)

    def body(i, carry):
        q0 = pl.multiple_of(i * qn, qn)
        q = qn_ref[pl.ds(q0, qn), :]
        ws = jnp.clip(i * NA_QROWS - NA_KH // 2, 0, rows - NA_KROWS)
        k0 = pl.multiple_of(ws * GRID_W, GRID_W)
        kw = kn_ref[pl.ds(k0, kn), :]
        vw = v_ref[pl.ds(k0, kn), :]
        variant = jnp.where(i == 0, 0, jnp.where(i == n_blk - 1, 2, 1))
        s_lat = _dot_nt(q, kw) + bias_ref[variant]
        store(q0, qn, softmax_pv([(s_lat, vw), (_dot_nt(q, kc), vc)]))
        return carry

    lax.fori_loop(0, n_blk, body, 0, unroll=2)
    qc = qn_ref[S:S + C, :]
    store(S, C, softmax_pv([(_dot_nt(qc, kc), vc)]))


def _neighbour_attn(p1, q_gain, k_gain, bias, l, *, B, S, C, scale):
    T = S + C
    col = lambda c0: (lambda b, h: (b, c0 // HEAD_DIM + h))
    gain_spec = pl.BlockSpec((1, LANES), lambda b, h: (0, 0))
    return pl.pallas_call(
        functools.partial(_neighbour_kernel, S=S, C=C, scale=scale),
        out_shape=jax.ShapeDtypeStruct((B, T, MIX_W), jnp.bfloat16),
        grid=(B, N_HEADS),
        in_specs=[
            pl.BlockSpec((T, HEAD_DIM), col(C1_BQ)),
            pl.BlockSpec((T, HEAD_DIM), col(C1_BK)),
            pl.BlockSpec((T, HEAD_DIM), col(C1_BV)),
            pl.BlockSpec((T, HEAD_DIM), col(C1_BG)),
            gain_spec, gain_spec,
            pl.BlockSpec((None, 3, None) + bias.shape[3:], lambda b, h: (l, 0, h, 0, 0)),
        ],
        out_specs=pl.BlockSpec((None, T, HEAD_DIM), lambda b, h: (b, 0, h)),
        scratch_shapes=[pltpu.VMEM((T, HEAD_DIM), jnp.bfloat16), pltpu.VMEM((T, HEAD_DIM), jnp.bfloat16)],
        compiler_params=_cparams("arbitrary", "arbitrary"),
        name="neighbour_attn",
    )(p1, p1, p1, p1, q_gain, k_gain, bias)


def _neighbour_bias(rpb, rows):
    W = GRID_W
    a = np.arange(NA_QROWS)
    j = np.arange(NA_KROWS)
    w = np.arange(W)
    cs = np.clip(w - NA_KW // 2, 0, W - NA_KW)
    col_ok = (w[None, :] >= cs[:, None]) & (w[None, :] < cs[:, None] + NA_KW)
    dc = np.clip(w[None, :] - w[:, None], -(NA_KW - 1), NA_KW - 1) + (NA_KW - 1)
    pick_dc = (dc[None] == np.arange(2 * NA_KW - 1)[:, None, None]).astype(np.float32)
    dr_all, ok_all = [], []
    for r0 in (0, NA_QROWS, rows - NA_QROWS):
        ws = int(np.clip(r0 - NA_KH // 2, 0, rows - NA_KROWS))
        rq = r0 + a
        kr = ws + j
        rs = np.clip(rq - NA_KH // 2, 0, rows - NA_KH)
        row_ok = (kr[None, :] >= rs[:, None]) & (kr[None, :] < rs[:, None] + NA_KH)
        dr_all.append(np.clip(kr[None, :] - rq[:, None] + (NA_KH - 1), 0, 2 * NA_KH - 2))
        ok_all.append(row_ok[:, None, :, None] & col_ok[None, :, None, :])
    dr = np.stack(dr_all)
    ok = np.stack(ok_all)
    rows_sel = rpb.astype(jnp.float32)[:, :, dr, :]
    val = jnp.einsum("lhvajd,dqk->lvhaqjk", rows_sel, pick_dc, precision=lax.Precision.HIGHEST)
    val = jnp.where(ok[None, :, None], val, NEG_INF)
    L, H = rpb.shape[:2]
    return val.reshape(L, 3, H, NA_QROWS * W, NA_KROWS * W)


def _flash_tile(q, k_ref, v_ref, s_ref, acc_ref, *, S, C, chunk, with_latent):
    rows = q.shape[0]
    acc = acc_ref.at[:rows]

    def scores(k0, n):
        return _dot_nt(q, k_ref[pl.ds(k0, n), :])

    def update(s, k0, n, m):
        vals = v_ref[pl.ds(k0, n), :]
        vals = jnp.concatenate([vals, jnp.ones_like(vals)], axis=1)
        mx = jnp.max(s, axis=-1, keepdims=True)
        if m is None:
            m_new = mx
            acc[...] = _dot(jnp.exp2(s - m_new).astype(vals.dtype), vals)
        else:
            m_new = jnp.maximum(m, mx)
            acc[...] = jnp.exp2(m - m_new) * acc[...] + _dot(jnp.exp2(s - m_new).astype(vals.dtype), vals)
        return m_new

    if with_latent:
        n_pairs = S // (2 * chunk)
        s_even, s_odd = s_ref.at[0, :rows], s_ref.at[1, :rows]
        s_even[...] = scores(0, chunk)
        m = update(scores(S, C), S, C, None)

        def pair(k0, m, last):
            s_odd[...] = scores(k0 + chunk, chunk)
            m = update(s_even[...], k0, chunk, m)
            if not last:
                s_even[...] = scores(k0 + 2 * chunk, chunk)
            return update(s_odd[...], k0 + chunk, chunk, m)

        m = lax.fori_loop(0, n_pairs - 1,
                          lambda i, m: pair(pl.multiple_of(i * 2 * chunk, 2 * chunk), m, False), m)
        pair((n_pairs - 1) * 2 * chunk, m, True)
    else:
        update(scores(S, C), S, C, None)
    return acc[:, :HEAD_DIM] / acc[:, HEAD_DIM:]


def _mla_kernel(q_ref, k_ref, v_ref, g_ref, o_ref, s_ref, acc_ref, *, S, C, tq, chunk):
    def tile(r0, n, with_latent):
        o = _flash_tile(q_ref[pl.ds(r0, n), :], k_ref, v_ref, s_ref, acc_ref,
                        S=S, C=C, chunk=chunk, with_latent=with_latent)
        g = g_ref[pl.ds(r0, n), :].astype(jnp.float32)
        o_ref[pl.ds(r0, n), :] = (o * _silu(g)).astype(o_ref.dtype)

    def body(i, carry):
        tile(pl.multiple_of(i * tq, tq), tq, True)
        return carry

    lax.fori_loop(0, S // tq, body, 0)
    tile(S, C, False)


def _flash_scratch(rows, chunk):
    return [pltpu.VMEM((2, rows, chunk), jnp.float32), pltpu.VMEM((rows, 2 * HEAD_DIM), jnp.float32)]


def _mla_attn(q, k, v, p2, *, S, C):
    B, T, _ = q.shape
    head = lambda b, h: (b, 0, h)
    tq, chunk = 512, min(1024, S // 2)
    return pl.pallas_call(
        functools.partial(_mla_kernel, S=S, C=C, tq=tq, chunk=chunk),
        scratch_shapes=_flash_scratch(tq, chunk),
        out_shape=jax.ShapeDtypeStruct((B, T, MIX_W), jnp.bfloat16),
        grid=(B, N_HEADS),
        in_specs=[
            pl.BlockSpec((None, T, MLA_QK), head),
            pl.BlockSpec((None, T, MLA_QK), head),
            pl.BlockSpec((None, T, HEAD_DIM), head),
            pl.BlockSpec((T, HEAD_DIM), lambda b, h: (b, C2_CG // HEAD_DIM + h)),
        ],
        out_specs=pl.BlockSpec((None, T, HEAD_DIM), head),
        compiler_params=_cparams("arbitrary", "arbitrary"),
        name="mla_attn",
    )(q, k, v, p2)


def _diff_kernel(q_ref, k_ref, v_ref, g_ref, lam_ref, sub_ref, o_ref, s_ref, acc_ref,
                 *, S, C, tq, chunk, lambda_init):
    lv = lam_ref[...]
    lam = (jnp.exp(jnp.sum(lv[0:1] * lv[1:2], axis=-1, keepdims=True))
           - jnp.exp(jnp.sum(lv[2:3] * lv[3:4], axis=-1, keepdims=True)) + lambda_init)
    sub = sub_ref[...] * (1.0 - lambda_init)

    def tile(r0, n, with_latent):
        q = q_ref[pl.ds(r0, n), :]
        low = lax.broadcasted_iota(jnp.int32, q.shape, 1) < DIFF_DIM
        zero = jnp.zeros_like(q)
        q2 = jnp.concatenate([jnp.where(low, q, zero), jnp.where(low, zero, q)], axis=0)
        o = _flash_tile(q2, k_ref, v_ref, s_ref, acc_ref, S=S, C=C, chunk=chunk, with_latent=with_latent)
        d = o[:n] - lam * o[n:]
        ms = jnp.mean(d * d, axis=-1, keepdims=True)
        g = g_ref[pl.ds(r0, n), :].astype(jnp.float32)
        o_ref[pl.ds(r0, n), :] = (d * lax.rsqrt(ms + EPS) * sub * _silu(g)).astype(o_ref.dtype)

    def body(i, carry):
        tile(pl.multiple_of(i * tq, tq), tq, True)
        return carry

    lax.fori_loop(0, S // tq, body, 0)
    tile(S, C, False)


def _diff_attn(q, k, p2, lam_rows, subln, *, S, C, lambda_init):
    B, T, _ = q.shape
    head = lambda b, h: (b, 0, h)
    const = lambda b, h: (0, 0)
    tq, chunk = 256, min(1024, S // 2)
    return pl.pallas_call(
        functools.partial(_diff_kernel, S=S, C=C, tq=tq, chunk=chunk, lambda_init=lambda_init),
        scratch_shapes=_flash_scratch(2 * tq, chunk),
        out_shape=jax.ShapeDtypeStruct((B, T, MIX_W), jnp.bfloat16),
        grid=(B, N_HEADS),
        in_specs=[
            pl.BlockSpec((None, T, HEAD_DIM), head),
            pl.BlockSpec((None, T, HEAD_DIM), head),
            pl.BlockSpec((T, HEAD_DIM), lambda b, h: (b, C2_DV // HEAD_DIM + h)),
            pl.BlockSpec((T, HEAD_DIM), lambda b, h: (b, C2_DG // HEAD_DIM + h)),
            pl.BlockSpec((8, LANES), const),
            pl.BlockSpec((1, LANES), const),
        ],
        out_specs=pl.BlockSpec((None, T, HEAD_DIM), head),
        compiler_params=_cparams("arbitrary", "arbitrary"),
        name="diff_attn",
    )(q, k, p2, p2, lam_rows, subln)


def _merge_kernel(ua_ref, ub_ref, uc_ref, ud_ref, m0_ref, m1_ref, m2_ref, m3_ref, w_ref, o_ref):
    acc = None
    for i, (u, mg) in enumerate(zip((ua_ref, ub_ref, uc_ref, ud_ref), (m0_ref, m1_ref, m2_ref, m3_ref))):
        t = _sigmoid(mg[...].astype(jnp.float32)) * _dot(u[...], w_ref[i])
        acc = t if acc is None else acc + t
    o_ref[...] = acc.astype(o_ref.dtype)


def _merge(us, p2, wb, l):
    R = p2.shape[0]
    D = wb.shape[-1]
    tm, tn = _row_tile(R, 640), 1024
    u_spec = pl.BlockSpec((tm, MIX_W), lambda m, n: (m, 0))
    mg_specs = [pl.BlockSpec((tm, tn), functools.partial(lambda m, n, i: (m, (C2_MG + i * D) // tn + n), i=i))
                for i in range(4)]
    return pl.pallas_call(
        _merge_kernel,
        out_shape=jax.ShapeDtypeStruct((R, D), jnp.bfloat16),
        grid=(R // tm, D // tn),
        in_specs=[u_spec, u_spec, u_spec, u_spec, *mg_specs,
                  pl.BlockSpec((None, 4, MIX_W, tn), lambda m, n: (l, 0, 0, n))],
        out_specs=pl.BlockSpec((tm, tn), lambda m, n: (m, n)),
        compiler_params=_cparams("arbitrary", "arbitrary"),
        name="merge",
    )(*[u.reshape(R, MIX_W) for u in us], p2, p2, p2, p2, wb)


def _out_kernel(a_ref, w_ref, x_ref, gt_ref, o_ref):
    o_ref[...] = x_ref[...] + gt_ref[...] * _dot(a_ref[...], w_ref[...])


def _out_proj(merged, w, xs, mods4, l, S, out_rows):
    B, T, D = xs.shape
    tn = 1024
    row = _mod_row(S, B)
    return pl.pallas_call(
        _out_kernel,
        out_shape=jax.ShapeDtypeStruct((B, out_rows, D), jnp.float32),
        grid=(D // tn, B, out_rows // ROW_TILE),
        in_specs=[
            pl.BlockSpec((None, ROW_TILE, D), lambda n, b, t: (b, t, 0)),
            pl.BlockSpec((None, D, tn), lambda n, b, t: (l, 0, n)),
            pl.BlockSpec((None, ROW_TILE, tn), lambda n, b, t: (b, t, n)),
            pl.BlockSpec((None, None, 1, tn), lambda n, b, t: (l, row(b, t), 0, 2 * (D // tn) + n)),
        ],
        out_specs=pl.BlockSpec((None, ROW_TILE, tn), lambda n, b, t: (b, t, n)),
        compiler_params=_cparams("arbitrary", "arbitrary", "arbitrary"),
        name="out_proj",
    )(merged.reshape(B, T, D), w, xs, mods4)


def _rope_tables(S, C, d_rot):
    t = jnp.arange(S, dtype=jnp.int32)
    row = (t // GRID_W).astype(jnp.float32)
    col = (t % GRID_W).astype(jnp.float32)
    n_f = d_rot // 4
    inv = jnp.power(ROPE_BASE, -jnp.arange(n_f, dtype=jnp.float32) / n_f)
    ang = jnp.concatenate([row[:, None] * inv, col[:, None] * inv], axis=-1)
    cos, sin = jnp.cos(ang), jnp.sin(ang)
    zero = jnp.zeros_like(sin)
    reps = LANES // d_rot
    cos_t = jnp.tile(jnp.concatenate([cos, cos], axis=-1), (1, reps))
    sa_t = jnp.tile(jnp.concatenate([-sin, zero], axis=-1), (1, reps))
    sb_t = jnp.tile(jnp.concatenate([zero, sin], axis=-1), (1, reps))
    pad = lambda a, v: jnp.concatenate([a, jnp.full((C, LANES), v, jnp.float32)], axis=0)
    return pad(cos_t, 1.0), pad(sa_t, 0.0), pad(sb_t, 0.0)


def _split_w_q_up(w):
    L, K, _ = w.shape
    w = w.reshape(L, K, N_HEADS, MLA_NOPE + MLA_ROPE)
    w = jnp.pad(w, ((0, 0), (0, 0), (0, 0), (0, MLA_QK - MLA_NOPE - MLA_ROPE)))
    return w.reshape(L, K, N_HEADS * MLA_QK).astype(jnp.bfloat16)


def _split_w_kv_up(w):
    L, K, _ = w.shape
    w = w.reshape(L, K, N_HEADS, 2, HEAD_DIM)
    return jnp.swapaxes(w, 2, 3).reshape(L, K, 2 * MIX_W).astype(jnp.bfloat16)


def _lane_row(v, reps=1, pad=0):
    r = jnp.tile(v.astype(jnp.float32), reps)
    return jnp.pad(r, (0, pad)).reshape(1, LANES)


def kernel(x, c, ctx, c_ctx, w_ada, b_ada, g_norm, w_in, qn_a, kn_a, sink_a, qn_b, kn_b, rpb_b, g_qa, g_kva,
           w_q_up, w_kv_up, qn_nope, qn_pe, kn_nope, kn_pe, qn_d, kn_d, lam_q1, lam_k1, lam_q2, lam_k2,
           subln_d, w_branch, w_out):
    B, S, D = x.shape
    C = ctx.shape[1]
    L = w_ada.shape[0]
    T = S + C
    R = B * T
    rows = S // GRID_W
    assert D == 4 * MIX_W and S % 1024 == 0 and C == ROW_TILE and B + 1 <= 8
    assert rows % NA_QROWS == 0 and rows >= NA_KROWS + NA_QROWS
    assert w_in.shape[-1] == P2_START + C2_MG + 4 * D

    xs = jnp.concatenate([x, ctx], axis=1)
    cc = jnp.zeros((8, D), jnp.float32).at[:B].set(c).at[B].set(c_ctx)
    mods4 = _ada(cc, w_ada, b_ada).reshape(L, 8, 1, 3 * D)

    w1 = w_in[:, :, :P1_W].astype(jnp.bfloat16)
    w2 = w_in[:, :, P2_START:].astype(jnp.bfloat16)
    w_q_p = _split_w_q_up(w_q_up)
    w_kv_p = _split_w_kv_up(w_kv_up)
    w_branch_p = w_branch.astype(jnp.bfloat16)
    w_out_p = w_out.astype(jnp.bfloat16)
    tab_a = _rope_tables(S, C, HEAD_DIM)
    tab_d = _rope_tables(S, C, DIFF_DIM)
    bias_b = _neighbour_bias(rpb_b, rows)

    for l in range(L):
        lambda_init = 0.8 - 0.6 * math.exp(-0.3 * l)
        h = _prologue(xs, g_norm[l], mods4, l, S).reshape(R, D)
        p1 = _in_proj(h, w1, l, "in_proj_1")
        p2 = _in_proj(h, w2, l, "in_proj_2")

        qa, ka = _normrope(p1, C1_AQ, MIX_W, C1_AK, WA_KV_HEADS * HEAD_DIM, _lane_row(qn_a[l]), _lane_row(kn_a[l]),
                           tab_a, B=B, T=T, group=HEAD_DIM, half=HEAD_DIM // 2, scale=HEAD_DIM ** -0.5,
                           name="prep_window")
        sink = jnp.broadcast_to(sink_a[l].astype(jnp.float32)[:, None, None], (N_HEADS, 1, LANES))
        ua = _window_attn(qa, ka, p1, sink, S=S, C=C)

        ub = _neighbour_attn(p1, _lane_row(qn_b[l]), _lane_row(kn_b[l]), bias_b, l,
                             B=B, S=S, C=C, scale=HEAD_DIM ** -0.5)

        qc, kc, vc = _mla_prep(p1, g_qa[l].reshape(1, -1), g_kva[l].reshape(1, -1), w_q_p, w_kv_p, l,
                               _lane_row(qn_nope[l]), _lane_row(qn_pe[l], pad=LANES - MLA_ROPE),
                               _lane_row(kn_nope[l]), _lane_row(kn_pe[l], pad=LANES - MLA_ROPE),
                               tab_d, B=B, T=T, scale=(MLA_NOPE + MLA_ROPE) ** -0.5 * LOG2E)
        uc = _mla_attn(qc, kc, vc, p2, S=S, C=C)

        qd, kd = _normrope(p2, C2_DQ, MIX_W, C2_DK, MIX_W, _lane_row(qn_d[l], reps=2), _lane_row(kn_d[l], reps=2),
                           tab_d, B=B, T=T, group=DIFF_DIM, half=DIFF_DIM // 2, scale=DIFF_DIM ** -0.5 * LOG2E,
                           name="prep_diff")
        lam_rows = jnp.zeros((8, LANES), jnp.float32).at[:4, :DIFF_DIM].set(
            jnp.stack([lam_q1[l], lam_k1[l], lam_q2[l], lam_k2[l]]).astype(jnp.float32))
        ud = _diff_attn(qd, kd, p2, lam_rows, _lane_row(subln_d[l]), S=S, C=C, lambda_init=lambda_init)

        merged = _merge((ua, ub, uc, ud), p2, w_branch_p, l)
        xs = _out_proj(merged, w_out_p, xs, mods4, l, S, S if l == L - 1 else T)
    return xs
```

```python
import functools
import math

import numpy as np
import jax
import jax.numpy as jnp
from jax import lax
from jax.experimental import pallas as pl
from jax.experimental.pallas import tpu as pltpu

GRID_W = 64
HEAD_DIM = 128
N_HEADS = 8
MIX_W = N_HEADS * HEAD_DIM
ROPE_BASE = 10000.0
NEG_INF = -1e30
EPS = 1e-6
LOG2E = math.log2(math.e)
WA_KV_HEADS = 2
WA_GROUP = N_HEADS // WA_KV_HEADS
WINDOW = 128
NA_KH = 8
NA_KW = 16
NA_QROWS = 4
NA_KROWS = NA_QROWS + NA_KH
MLA_Q_LORA = 896
MLA_KV_LORA = 512
MLA_NOPE = 128
MLA_ROPE = 64
MLA_QK = 256
DIFF_DIM = 64

LANES = 128
BF16_ROWS = 16
ROW_TILE = 256
VMEM_LIMIT = 56 * 1024 * 1024

P1_W = 8192
P2_START = 8128
C1_AQ, C1_AK, C1_AV, C1_AG = 0, 1024, 1280, 1536
C1_BQ, C1_BK, C1_BV, C1_BG = 2560, 3584, 4608, 5632
C1_MLA, MLA_IN_W = 6144, 2048
X_CQ, X_CKV, X_KPE = 512, 1408, 1920
C2_CG, C2_DQ, C2_DK, C2_DV, C2_DG, C2_MG = 0, 1024, 2048, 3072, 4096, 5120


def _cparams(*sem):
    return pltpu.CompilerParams(dimension_semantics=sem, vmem_limit_bytes=VMEM_LIMIT)


def _dot(a, b):
    return jnp.dot(a, b, preferred_element_type=jnp.float32)


def _dot_nt(a, b):
    return lax.dot_general(a, b, (((1,), (1,)), ((), ())), preferred_element_type=jnp.float32)


def _sigmoid(v):
    return 0.5 * jnp.tanh(0.5 * v) + 0.5


def _silu(v):
    return v * _sigmoid(v)


def _row_tile(n, cap):
    return max(t for t in range(BF16_ROWS, cap + 1, BF16_ROWS) if n % t == 0)


def _ada_kernel(c_ref, w_ref, b_ref, o_ref):
    s = _silu(c_ref[...]).astype(jnp.bfloat16)
    o_ref[...] = _dot(s, w_ref[...].astype(jnp.bfloat16)) + b_ref[...]


def _ada(cc, w_ada, b_ada):
    L, D, N = w_ada.shape
    tn = 512
    return pl.pallas_call(
        _ada_kernel,
        out_shape=jax.ShapeDtypeStruct((L, 8, N), jnp.float32),
        grid=(L, N // tn),
        in_specs=[
            pl.BlockSpec((8, D), lambda l, n: (0, 0)),
            pl.BlockSpec((None, D, tn), lambda l, n: (l, 0, n)),
            pl.BlockSpec((None, 1, tn), lambda l, n: (l, 0, n)),
        ],
        out_specs=pl.BlockSpec((None, 8, tn), lambda l, n: (l, 0, n)),
        compiler_params=_cparams("arbitrary", "arbitrary"),
        name="ada_mod",
    )(cc, w_ada, b_ada.reshape(L, 1, N))


def _prologue_kernel(x_ref, g_ref, sh_ref, sc_ref, o_ref):
    x = x_ref[...]
    ms = jnp.mean(x * x, axis=-1, keepdims=True)
    y = x * lax.rsqrt(ms + EPS) * g_ref[...]
    o_ref[...] = (y * (1.0 + sc_ref[...]) + sh_ref[...]).astype(o_ref.dtype)


def _mod_row(S, B):
    return lambda b, t: jnp.where(t >= S // ROW_TILE, B, b)


def _prologue(xs, g, mods4, l, S):
    B, T, D = xs.shape
    row = _mod_row(S, B)
    return pl.pallas_call(
        _prologue_kernel,
        out_shape=jax.ShapeDtypeStruct((B, T, D), jnp.bfloat16),
        grid=(B, T // ROW_TILE),
        in_specs=[
            pl.BlockSpec((None, ROW_TILE, D), lambda b, t: (b, t, 0)),
            pl.BlockSpec((1, D), lambda b, t: (0, 0)),
            pl.BlockSpec((None, None, 1, D), lambda b, t: (l, row(b, t), 0, 0)),
            pl.BlockSpec((None, None, 1, D), lambda b, t: (l, row(b, t), 0, 1)),
        ],
        out_specs=pl.BlockSpec((None, ROW_TILE, D), lambda b, t: (b, t, 0)),
        compiler_params=_cparams("arbitrary", "arbitrary"),
        name="norm_modulate",
    )(xs, g.reshape(1, D), mods4, mods4)


def _matmul_kernel(a_ref, w_ref, o_ref):
    o_ref[...] = _dot(a_ref[...], w_ref[...]).astype(o_ref.dtype)


def _in_proj(h, w, l, name):
    R, D = h.shape
    N = w.shape[2]
    tm, tn = _row_tile(R, 1100), 1024
    return pl.pallas_call(
        _matmul_kernel,
        out_shape=jax.ShapeDtypeStruct((R, N), jnp.bfloat16),
        grid=(N // tn, R // tm),
        in_specs=[
            pl.BlockSpec((tm, D), lambda n, m: (m, 0)),
            pl.BlockSpec((None, D, tn), lambda n, m: (l, 0, n)),
        ],
        out_specs=pl.BlockSpec((tm, tn), lambda n, m: (m, n)),
        compiler_params=_cparams("arbitrary", "arbitrary"),
        name=name,
    )(h, w)


def _rms128(x, gain, group, count=None):
    sq = x * x
    tot = jnp.sum(sq, axis=-1, keepdims=True)
    if group == LANES:
        inv = lax.rsqrt(tot / (count or group) + EPS)
    else:
        lane = lax.broadcasted_iota(jnp.int32, x.shape, 1)
        low = lane < group
        lo = jnp.sum(jnp.where(low, sq, 0.0), axis=-1, keepdims=True)
        inv = jnp.where(low, lax.rsqrt(lo / group + EPS), lax.rsqrt((tot - lo) / group + EPS))
    return x * inv * gain


def _rope128(x, cos, sa, sb, half):
    return x * cos + pltpu.roll(x, LANES - half, 1) * sa + pltpu.roll(x, half, 1) * sb


def _normrope_kernel(q_ref, k_ref, qg_ref, kg_ref, cos_ref, sa_ref, sb_ref, qo_ref, ko_ref,
                     *, group, half, scale):
    cos, sa, sb = cos_ref[...], sa_ref[...], sb_ref[...]
    for src, gain, dst, mul in ((q_ref, qg_ref, qo_ref, scale), (k_ref, kg_ref, ko_ref, None)):
        g = gain[...]
        for j in range(src.shape[-1] // LANES):
            sl = slice(j * LANES, (j + 1) * LANES)
            y = _rope128(_rms128(src[:, sl].astype(jnp.float32), g, group), cos, sa, sb, half)
            if mul is not None:
                y = y * mul
            dst[:, sl] = y.astype(dst.dtype)


def _normrope(p, col_q, wq, col_k, wk, q_gain, k_gain, tables, *, B, T, group, half, scale, name):
    nt = T // ROW_TILE
    tab_spec = pl.BlockSpec((ROW_TILE, LANES), lambda b, t: (t, 0))
    gain_spec = pl.BlockSpec((1, LANES), lambda b, t: (0, 0))
    return pl.pallas_call(
        functools.partial(_normrope_kernel, group=group, half=half, scale=scale),
        out_shape=(jax.ShapeDtypeStruct((B, T, wq), jnp.bfloat16),
                   jax.ShapeDtypeStruct((B, T, wk), jnp.bfloat16)),
        grid=(B, nt),
        in_specs=[
            pl.BlockSpec((ROW_TILE, wq), lambda b, t: (b * nt + t, col_q // wq)),
            pl.BlockSpec((ROW_TILE, wk), lambda b, t: (b * nt + t, col_k // wk)),
            gain_spec, gain_spec, tab_spec, tab_spec, tab_spec,
        ],
        out_specs=(pl.BlockSpec((None, ROW_TILE, wq), lambda b, t: (b, t, 0)),
                   pl.BlockSpec((None, ROW_TILE, wk), lambda b, t: (b, t, 0))),
        compiler_params=_cparams("arbitrary", "arbitrary"),
        name=name,
    )(p, p, q_gain, k_gain, *tables)


def _mla_prep_kernel(x_ref, gqa_ref, gkva_ref, wq_ref, wkv_ref,
                     qnn_ref, qnp_ref, knn_ref, knp_ref, cos_ref, sa_ref, sb_ref,
                     q_ref, k_ref, v_ref, *, scale):
    cos, sa, sb = cos_ref[...], sa_ref[...], sb_ref[...]
    half = MLA_ROPE // 2

    def full_rms(v, gain_ref):
        v = v.astype(jnp.float32)
        ms = jnp.mean(v * v, axis=-1, keepdims=True)
        return (v * lax.rsqrt(ms + EPS) * gain_ref[...]).astype(jnp.bfloat16)

    cq = x_ref[:, X_CQ:X_CQ + MLA_Q_LORA]
    ckv = x_ref[:, X_CKV:X_CKV + MLA_KV_LORA]
    kpe = x_ref[:, X_KPE:X_KPE + LANES].astype(jnp.float32)
    kpe = jnp.where(lax.broadcasted_iota(jnp.int32, kpe.shape, 1) < MLA_ROPE, kpe, 0.0)
    qh = _dot(full_rms(cq, gqa_ref), wq_ref[...])
    kv = _dot(full_rms(ckv, gkva_ref), wkv_ref[...])
    kpe = _rope128(_rms128(kpe, knp_ref[...], LANES, MLA_ROPE), cos, sa, sb, half).astype(k_ref.dtype)
    qnn, qnp, knn = qnn_ref[...], qnp_ref[...], knn_ref[...]
    for h in range(N_HEADS):
        nope = slice(h * MLA_QK, h * MLA_QK + LANES)
        pe = slice(h * MLA_QK + LANES, (h + 1) * MLA_QK)
        q_ref[:, nope] = (_rms128(qh[:, nope], qnn, LANES) * scale).astype(q_ref.dtype)
        qpe = _rms128(qh[:, pe], qnp, LANES, MLA_ROPE)
        q_ref[:, pe] = (_rope128(qpe, cos, sa, sb, half) * scale).astype(q_ref.dtype)
        k_ref[:, nope] = _rms128(kv[:, h * LANES:(h + 1) * LANES], knn, LANES).astype(k_ref.dtype)
        k_ref[:, pe] = kpe
    v_ref[...] = kv[:, MIX_W:].astype(v_ref.dtype)


def _mla_prep(p1, g_qa, g_kva, wq, wkv, l, qnn, qnp, knn, knp, tables, *, B, T, scale):
    nt = T // ROW_TILE
    const = lambda b, t: (0, 0)
    gain_spec = pl.BlockSpec((1, LANES), const)
    tab_spec = pl.BlockSpec((ROW_TILE, LANES), lambda b, t: (t, 0))
    out_qk = jax.ShapeDtypeStruct((B, T, N_HEADS * MLA_QK), jnp.bfloat16)
    spec_qk = pl.BlockSpec((None, ROW_TILE, N_HEADS * MLA_QK), lambda b, t: (b, t, 0))
    return pl.pallas_call(
        functools.partial(_mla_prep_kernel, scale=scale),
        out_shape=(out_qk, out_qk, jax.ShapeDtypeStruct((B, T, MIX_W), jnp.bfloat16)),
        grid=(B, nt),
        in_specs=[
            pl.BlockSpec((ROW_TILE, MLA_IN_W), lambda b, t: (b * nt + t, C1_MLA // MLA_IN_W)),
            pl.BlockSpec((1, MLA_Q_LORA), const),
            pl.BlockSpec((1, MLA_KV_LORA), const),
            pl.BlockSpec((None,) + wq.shape[1:], lambda b, t: (l, 0, 0)),
            pl.BlockSpec((None,) + wkv.shape[1:], lambda b, t: (l, 0, 0)),
            gain_spec, gain_spec, gain_spec, gain_spec, tab_spec, tab_spec, tab_spec,
        ],
        out_specs=(spec_qk, spec_qk, pl.BlockSpec((None, ROW_TILE, MIX_W), lambda b, t: (b, t, 0))),
        compiler_params=_cparams("arbitrary", "arbitrary"),
        name="mla_prep",
    )(p1, g_qa, g_kva, wq, wkv, qnn, qnp, knn, knp, *tables)


def _window_kernel(q_ref, k_ref, v_ref, g_ref, sink_ref, o_ref, *, S, C):
    blk = WINDOW
    band = 3 * blk
    t = pl.program_id(2)
    kc = k_ref[S:S + C, :]
    vc = v_ref[S:S + C, :]

    def stack(x):
        return jnp.concatenate([x[:, r * HEAD_DIM:(r + 1) * HEAD_DIM] for r in range(WA_GROUP)], axis=0)

    def sink_col(n):
        return jnp.concatenate([jnp.broadcast_to(sink_ref[r:r + 1, :1], (n, 1)) for r in range(WA_GROUP)], axis=0)

    def attend(parts, sink):
        m = sink
        for s, _ in parts:
            m = jnp.maximum(m, jnp.max(s, axis=-1, keepdims=True))
        den = jnp.exp(sink - m)
        acc = None
        for s, v in parts:
            e = jnp.exp(s - m)
            den = den + jnp.sum(e, axis=-1, keepdims=True)
            pv = _dot(e.astype(v.dtype), v)
            acc = pv if acc is None else acc + pv
        return acc / den

    def store(r0, n, o):
        for r in range(WA_GROUP):
            cols = slice(r * HEAD_DIM, (r + 1) * HEAD_DIM)
            g = g_ref[r0:r0 + n, cols].astype(jnp.float32)
            o_ref[r0:r0 + n, cols] = (o[r * n:(r + 1) * n] * _silu(g)).astype(o_ref.dtype)

    @pl.when(t < S // ROW_TILE)
    def _():
        sink = sink_col(blk)
        for i in range(ROW_TILE // blk):
            q0 = t * ROW_TILE + i * blk
            q = stack(q_ref[i * blk:(i + 1) * blk, :])
            k0 = pl.multiple_of(jnp.clip(q0 - blk, 0, S - band), blk)
            kb = k_ref[pl.ds(k0, band), :]
            vb = v_ref[pl.ds(k0, band), :]
            s_lat = _dot_nt(q, kb)
            q_pos = q0 + (lax.broadcasted_iota(jnp.int32, s_lat.shape, 0) & (blk - 1))
            k_pos = k0 + lax.broadcasted_iota(jnp.int32, s_lat.shape, 1)
            s_lat = jnp.where(jnp.abs(q_pos - k_pos) <= WINDOW, s_lat, NEG_INF)
            store(i * blk, blk, attend([(s_lat, vb), (_dot_nt(q, kc), vc)], sink))

    @pl.when(t >= S // ROW_TILE)
    def _():
        q = stack(q_ref[...])
        store(0, ROW_TILE, attend([(_dot_nt(q, kc), vc)], sink_col(ROW_TILE)))


def _window_attn(q, k, p1, sink, *, S, C):
    B, T, _ = q.shape
    nt = T // ROW_TILE
    gw = WA_GROUP * HEAD_DIM
    return pl.pallas_call(
        functools.partial(_window_kernel, S=S, C=C),
        out_shape=jax.ShapeDtypeStruct((B, T, MIX_W), jnp.bfloat16),
        grid=(B, WA_KV_HEADS, nt),
        in_specs=[
            pl.BlockSpec((None, ROW_TILE, gw), lambda b, g, t: (b, t, g)),
            pl.BlockSpec((None, T, HEAD_DIM), lambda b, g, t: (b, 0, g)),
            pl.BlockSpec((T, HEAD_DIM), lambda b, g, t: (b, C1_AV // HEAD_DIM + g)),
            pl.BlockSpec((ROW_TILE, gw), lambda b, g, t: (b * nt + t, C1_AG // gw + g)),
            pl.BlockSpec((None, 8, LANES), lambda b, g, t: (g, 0, 0)),
        ],
        out_specs=pl.BlockSpec((None, ROW_TILE, gw), lambda b, g, t: (b, t, g)),
        compiler_params=_cparams("arbitrary", "arbitrary", "arbitrary"),
        name="window_attn",
    )(q, k, p1, p1, sink)


NB_HEADS = 2


def _neighbour_kernel(q_ref, k_ref, v_ref, g_ref, qg_ref, kg_ref, bias_ref, o_ref, kn_ref, *, S, C, scale):
    rows = S // GRID_W
    n_blk = rows // NA_QROWS
    kn = NA_KROWS * GRID_W
    T = S + C
    t = pl.program_id(2)

    @pl.when(t == 0)
    def _():
        def norm_body(i, carry):
            r0 = pl.multiple_of(i * ROW_TILE, ROW_TILE)
            for hh in range(NB_HEADS):
                cols = slice(hh * HEAD_DIM, (hh + 1) * HEAD_DIM)
                k = _rms128(k_ref[pl.ds(r0, ROW_TILE), cols].astype(jnp.float32), kg_ref[...], LANES)
                kn_ref[pl.ds(r0, ROW_TILE), cols] = k.astype(kn_ref.dtype)
            return carry

        lax.fori_loop(0, T // ROW_TILE, norm_body, 0)

    def softmax_pv(parts):
        m = None
        for s, _ in parts:
            mx = jnp.max(s, axis=-1, keepdims=True)
            m = mx if m is None else jnp.maximum(m, mx)
        den, acc = None, None
        for s, v in parts:
            e = jnp.exp(s - m)
            sm = jnp.sum(e, axis=-1, keepdims=True)
            pv = _dot(e.astype(v.dtype), v)
            den = sm if den is None else den + sm
            acc = pv if acc is None else acc + pv
        return acc / den

    def head(hh, latent):
        cols = slice(hh * HEAD_DIM, (hh + 1) * HEAD_DIM)
        q = (_rms128(q_ref[:, cols].astype(jnp.float32), qg_ref[...], LANES) * scale).astype(kn_ref.dtype)
        parts = [(_dot_nt(q, kn_ref[S:S + C, cols]), v_ref[S:S + C, cols])]
        if latent:
            ws = jnp.clip(t * NA_QROWS - NA_KH // 2, 0, rows - NA_KROWS)
            k0 = pl.multiple_of(ws * GRID_W, GRID_W)
            variant = jnp.where(t == 0, 0, jnp.where(t == n_blk - 1, 2, 1))
            s_lat = _dot_nt(q, kn_ref[pl.ds(k0, kn), cols]) + bias_ref[variant, hh]
            parts.insert(0, (s_lat, v_ref[pl.ds(k0, kn), cols]))
        g = g_ref[:, cols].astype(jnp.float32)
        o_ref[:, cols] = (softmax_pv(parts) * _silu(g)).astype(o_ref.dtype)

    @pl.when(t < n_blk)
    def _():
        for hh in range(NB_HEADS):
            head(hh, True)

    @pl.when(t >= n_blk)
    def _():
        for hh in range(NB_HEADS):
            head(hh, False)


def _neighbour_attn(p1, q_gain, k_gain, bias, l, *, B, S, C, scale):
    T = S + C
    nt = T // ROW_TILE
    bw = NB_HEADS * HEAD_DIM
    assert NA_QROWS * GRID_W == ROW_TILE
    gain_spec = pl.BlockSpec((1, LANES), lambda b, h, t: (0, 0))
    tile = lambda c0: pl.BlockSpec((ROW_TILE, bw), lambda b, h, t: (b * nt + t, c0 // bw + h))
    full = lambda c0: pl.BlockSpec((T, bw), lambda b, h, t: (b, c0 // bw + h))
    return pl.pallas_call(
        functools.partial(_neighbour_kernel, S=S, C=C, scale=scale),
        out_shape=jax.ShapeDtypeStruct((B, T, MIX_W), jnp.bfloat16),
        grid=(B, N_HEADS // NB_HEADS, nt),
        in_specs=[
            tile(C1_BQ), full(C1_BK), full(C1_BV), tile(C1_BG), gain_spec, gain_spec,
            pl.BlockSpec((None, 3, NB_HEADS) + bias.shape[3:], lambda b, h, t: (l, 0, h, 0, 0)),
        ],
        out_specs=pl.BlockSpec((None, ROW_TILE, bw), lambda b, h, t: (b, t, h)),
        scratch_shapes=[pltpu.VMEM((T, bw), jnp.bfloat16)],
        compiler_params=_cparams("arbitrary", "arbitrary", "arbitrary"),
        name="neighbour_attn",
    )(p1, p1, p1, p1, q_gain, k_gain, bias)


def _neighbour_bias(rpb, rows):
    W = GRID_W
    a = np.arange(NA_QROWS)
    j = np.arange(NA_KROWS)
    w = np.arange(W)
    cs = np.clip(w - NA_KW // 2, 0, W - NA_KW)
    col_ok = (w[None, :] >= cs[:, None]) & (w[None, :] < cs[:, None] + NA_KW)
    dc = np.clip(w[None, :] - w[:, None], -(NA_KW - 1), NA_KW - 1) + (NA_KW - 1)
    pick_dc = (dc[None] == np.arange(2 * NA_KW - 1)[:, None, None]).astype(np.float32)
    dr_all, ok_all = [], []
    for r0 in (0, NA_QROWS, rows - NA_QROWS):
        ws = int(np.clip(r0 - NA_KH // 2, 0, rows - NA_KROWS))
        rq = r0 + a
        kr = ws + j
        rs = np.clip(rq - NA_KH // 2, 0, rows - NA_KH)
        row_ok = (kr[None, :] >= rs[:, None]) & (kr[None, :] < rs[:, None] + NA_KH)
        dr_all.append(np.clip(kr[None, :] - rq[:, None] + (NA_KH - 1), 0, 2 * NA_KH - 2))
        ok_all.append(row_ok[:, None, :, None] & col_ok[None, :, None, :])
    dr = np.stack(dr_all)
    ok = np.stack(ok_all)
    rows_sel = rpb.astype(jnp.float32)[:, :, dr, :]
    val = jnp.einsum("lhvajd,dqk->lvhaqjk", rows_sel, pick_dc, precision=lax.Precision.HIGHEST)
    val = jnp.where(ok[None, :, None], val, NEG_INF)
    L, H = rpb.shape[:2]
    return val.reshape(L, 3, H, NA_QROWS * W, NA_KROWS * W)


def _flash_tile(q, k_ref, v_ref, s_ref, acc_ref, *, S, C, chunk, with_latent):
    rows = q.shape[0]
    acc = acc_ref.at[:rows]

    def scores(k0, n):
        return _dot_nt(q, k_ref[pl.ds(k0, n), :])

    def update(s, k0, n, m):
        vals = v_ref[pl.ds(k0, n), :]
        vals = jnp.concatenate([vals, jnp.ones_like(vals)], axis=1)
        mx = jnp.max(s, axis=-1, keepdims=True)
        if m is None:
            m_new = mx
            acc[...] = _dot(jnp.exp2(s - m_new).astype(vals.dtype), vals)
        else:
            m_new = jnp.maximum(m, mx)
            acc[...] = jnp.exp2(m - m_new) * acc[...] + _dot(jnp.exp2(s - m_new).astype(vals.dtype), vals)
        return m_new

    if with_latent:
        n_pairs = S // (2 * chunk)
        s_even, s_odd = s_ref.at[0, :rows], s_ref.at[1, :rows]
        s_even[...] = scores(0, chunk)
        m = update(scores(S, C), S, C, None)

        def pair(k0, m, last):
            s_odd[...] = scores(k0 + chunk, chunk)
            m = update(s_even[...], k0, chunk, m)
            if not last:
                s_even[...] = scores(k0 + 2 * chunk, chunk)
            return update(s_odd[...], k0 + chunk, chunk, m)

        m = lax.fori_loop(0, n_pairs - 1,
                          lambda i, m: pair(pl.multiple_of(i * 2 * chunk, 2 * chunk), m, False), m)
        pair((n_pairs - 1) * 2 * chunk, m, True)
    else:
        update(scores(S, C), S, C, None)
    return acc[:, :HEAD_DIM] / acc[:, HEAD_DIM:]


def _mla_kernel(q_ref, k_ref, v_ref, g_ref, o_ref, s_ref, acc_ref, *, S, C, tq, chunk):
    def tile(r0, n, with_latent):
        o = _flash_tile(q_ref[pl.ds(r0, n), :], k_ref, v_ref, s_ref, acc_ref,
                        S=S, C=C, chunk=chunk, with_latent=with_latent)
        g = g_ref[pl.ds(r0, n), :].astype(jnp.float32)
        o_ref[pl.ds(r0, n), :] = (o * _silu(g)).astype(o_ref.dtype)

    def body(i, carry):
        tile(pl.multiple_of(i * tq, tq), tq, True)
        return carry

    lax.fori_loop(0, S // tq, body, 0)
    tile(S, C, False)


def _flash_scratch(rows, chunk):
    return [pltpu.VMEM((2, rows, chunk), jnp.float32), pltpu.VMEM((rows, 2 * HEAD_DIM), jnp.float32)]


def _mla_attn(q, k, v, p2, *, S, C):
    B, T, _ = q.shape
    head = lambda b, h: (b, 0, h)
    tq, chunk = 1024, min(1024, S // 2)
    return pl.pallas_call(
        functools.partial(_mla_kernel, S=S, C=C, tq=tq, chunk=chunk),
        scratch_shapes=_flash_scratch(tq, chunk),
        out_shape=jax.ShapeDtypeStruct((B, T, MIX_W), jnp.bfloat16),
        grid=(B, N_HEADS),
        in_specs=[
            pl.BlockSpec((None, T, MLA_QK), head),
            pl.BlockSpec((None, T, MLA_QK), head),
            pl.BlockSpec((None, T, HEAD_DIM), head),
            pl.BlockSpec((T, HEAD_DIM), lambda b, h: (b, C2_CG // HEAD_DIM + h)),
        ],
        out_specs=pl.BlockSpec((None, T, HEAD_DIM), head),
        compiler_params=_cparams("arbitrary", "arbitrary"),
        name="mla_attn",
    )(q, k, v, p2)


def _diff_kernel(q_ref, k_ref, v_ref, g_ref, lam_ref, sub_ref, o_ref, s_ref, acc_ref,
                 *, S, C, tq, chunk, lambda_init):
    lv = lam_ref[...]
    lam = (jnp.exp(jnp.sum(lv[0:1] * lv[1:2], axis=-1, keepdims=True))
           - jnp.exp(jnp.sum(lv[2:3] * lv[3:4], axis=-1, keepdims=True)) + lambda_init)
    sub = sub_ref[...] * (1.0 - lambda_init)

    def tile(r0, n, with_latent):
        q = q_ref[pl.ds(r0, n), :]
        low = lax.broadcasted_iota(jnp.int32, q.shape, 1) < DIFF_DIM
        zero = jnp.zeros_like(q)
        q2 = jnp.concatenate([jnp.where(low, q, zero), jnp.where(low, zero, q)], axis=0)
        o = _flash_tile(q2, k_ref, v_ref, s_ref, acc_ref, S=S, C=C, chunk=chunk, with_latent=with_latent)
        d = o[:n] - lam * o[n:]
        ms = jnp.mean(d * d, axis=-1, keepdims=True)
        g = g_ref[pl.ds(r0, n), :].astype(jnp.float32)
        o_ref[pl.ds(r0, n), :] = (d * lax.rsqrt(ms + EPS) * sub * _silu(g)).astype(o_ref.dtype)

    def body(i, carry):
        tile(pl.multiple_of(i * tq, tq), tq, True)
        return carry

    lax.fori_loop(0, S // tq, body, 0)
    tile(S, C, False)


def _diff_attn(q, k, p2, lam_rows, subln, *, S, C, lambda_init):
    B, T, _ = q.shape
    head = lambda b, h: (b, 0, h)
    const = lambda b, h: (0, 0)
    tq, chunk = 512, min(1024, S // 2)
    return pl.pallas_call(
        functools.partial(_diff_kernel, S=S, C=C, tq=tq, chunk=chunk, lambda_init=lambda_init),
        scratch_shapes=_flash_scratch(2 * tq, chunk),
        out_shape=jax.ShapeDtypeStruct((B, T, MIX_W), jnp.bfloat16),
        grid=(B, N_HEADS),
        in_specs=[
            pl.BlockSpec((None, T, HEAD_DIM), head),
            pl.BlockSpec((None, T, HEAD_DIM), head),
            pl.BlockSpec((T, HEAD_DIM), lambda b, h: (b, C2_DV // HEAD_DIM + h)),
            pl.BlockSpec((T, HEAD_DIM), lambda b, h: (b, C2_DG // HEAD_DIM + h)),
            pl.BlockSpec((8, LANES), const),
            pl.BlockSpec((1, LANES), const),
        ],
        out_specs=pl.BlockSpec((None, T, HEAD_DIM), head),
        compiler_params=_cparams("arbitrary", "arbitrary"),
        name="diff_attn",
    )(q, k, p2, p2, lam_rows, subln)


def _merge_kernel(ua_ref, ub_ref, uc_ref, ud_ref, m0_ref, m1_ref, m2_ref, m3_ref, w_ref, o_ref):
    acc = None
    for i, (u, mg) in enumerate(zip((ua_ref, ub_ref, uc_ref, ud_ref), (m0_ref, m1_ref, m2_ref, m3_ref))):
        t = _sigmoid(mg[...].astype(jnp.float32)) * _dot(u[...], w_ref[i])
        acc = t if acc is None else acc + t
    o_ref[...] = acc.astype(o_ref.dtype)


def _merge(us, p2, wb, l):
    R = p2.shape[0]
    D = wb.shape[-1]
    tm, tn = _row_tile(R, 640), 1024
    u_spec = pl.BlockSpec((tm, MIX_W), lambda m, n: (m, 0))
    mg_specs = [pl.BlockSpec((tm, tn), functools.partial(lambda m, n, i: (m, (C2_MG + i * D) // tn + n), i=i))
                for i in range(4)]
    return pl.pallas_call(
        _merge_kernel,
        out_shape=jax.ShapeDtypeStruct((R, D), jnp.bfloat16),
        grid=(R // tm, D // tn),
        in_specs=[u_spec, u_spec, u_spec, u_spec, *mg_specs,
                  pl.BlockSpec((None, 4, MIX_W, tn), lambda m, n: (l, 0, 0, n))],
        out_specs=pl.BlockSpec((tm, tn), lambda m, n: (m, n)),
        compiler_params=_cparams("arbitrary", "arbitrary"),
        name="merge",
    )(*[u.reshape(R, MIX_W) for u in us], p2, p2, p2, p2, wb)


def _out_kernel(a_ref, w_ref, x_ref, gl_ref, gc_ref, o_ref, *, S):
    tm = o_ref.shape[0]
    row = pl.program_id(2) * tm + lax.broadcasted_iota(jnp.int32, o_ref.shape, 0)
    gate = jnp.where(row < S, gl_ref[...], gc_ref[...])
    o_ref[...] = x_ref[...] + gate * _dot(a_ref[...], w_ref[...])


def _out_proj(merged, w, xs, mods4, l, S, out_rows):
    B, T, D = xs.shape
    tm, tn = _row_tile(out_rows, 640), 1024
    gate_col = lambda n: 2 * (D // tn) + n
    return pl.pallas_call(
        functools.partial(_out_kernel, S=S),
        out_shape=jax.ShapeDtypeStruct((B, out_rows, D), jnp.float32),
        grid=(D // tn, B, out_rows // tm),
        in_specs=[
            pl.BlockSpec((None, tm, D), lambda n, b, t: (b, t, 0)),
            pl.BlockSpec((None, D, tn), lambda n, b, t: (l, 0, n)),
            pl.BlockSpec((None, tm, tn), lambda n, b, t: (b, t, n)),
            pl.BlockSpec((None, None, 1, tn), lambda n, b, t: (l, b, 0, gate_col(n))),
            pl.BlockSpec((None, None, 1, tn), lambda n, b, t: (l, B, 0, gate_col(n))),
        ],
        out_specs=pl.BlockSpec((None, tm, tn), lambda n, b, t: (b, t, n)),
        compiler_params=_cparams("arbitrary", "arbitrary", "arbitrary"),
        name="out_proj",
    )(merged.reshape(B, T, D), w, xs, mods4, mods4)


CAST_ROWS = 512
CAST_COLS = 1024


def _cast_kernel(x_ref, o_ref):
    o_ref[...] = x_ref[...].astype(o_ref.dtype)


def _cast_range1(w):
    L, D, _ = w.shape
    return pl.pallas_call(
        _cast_kernel,
        out_shape=jax.ShapeDtypeStruct((L, D, P1_W), jnp.bfloat16),
        grid=(L, D // CAST_ROWS, P1_W // CAST_COLS),
        in_specs=[pl.BlockSpec((None, CAST_ROWS, CAST_COLS), lambda l, r, n: (l, r, n))],
        out_specs=pl.BlockSpec((None, CAST_ROWS, CAST_COLS), lambda l, r, n: (l, r, n)),
        compiler_params=_cparams("arbitrary", "arbitrary", "arbitrary"),
        name="cast_w_in_1",
    )(w)


def _cast_shift_kernel(*refs):
    o_ref = refs[-1]
    half = LANES // 2
    low = lax.broadcasted_iota(jnp.int32, (CAST_ROWS, LANES), 1) < half
    rolled = [pltpu.roll(r[...], half, 1) for r in refs[:-1]]
    for i in range(len(rolled) - 1):
        o_ref[:, i * LANES:(i + 1) * LANES] = jnp.where(low, rolled[i], rolled[i + 1]).astype(o_ref.dtype)


def _cast_range2(w):
    L, D, n_in = w.shape
    n_out = n_in - P2_START
    first = (P2_START - LANES // 2) // LANES
    per = CAST_COLS // LANES
    assert (P2_START - LANES // 2) % LANES == 0 and n_out % CAST_COLS == 0
    in_specs = [pl.BlockSpec((None, CAST_ROWS, LANES),
                             functools.partial(lambda l, r, n, i: (l, r, first + per * n + i), i=i))
                for i in range(per + 1)]
    return pl.pallas_call(
        _cast_shift_kernel,
        out_shape=jax.ShapeDtypeStruct((L, D, n_out), jnp.bfloat16),
        grid=(L, D // CAST_ROWS, n_out // CAST_COLS),
        in_specs=in_specs,
        out_specs=pl.BlockSpec((None, CAST_ROWS, CAST_COLS), lambda l, r, n: (l, r, n)),
        compiler_params=_cparams("arbitrary", "arbitrary", "arbitrary"),
        name="cast_w_in_2",
    )(*([w] * (per + 1)))


def _rope_tables(S, C, d_rot):
    t = jnp.arange(S, dtype=jnp.int32)
    row = (t // GRID_W).astype(jnp.float32)
    col = (t % GRID_W).astype(jnp.float32)
    n_f = d_rot // 4
    inv = jnp.power(ROPE_BASE, -jnp.arange(n_f, dtype=jnp.float32) / n_f)
    ang = jnp.concatenate([row[:, None] * inv, col[:, None] * inv], axis=-1)
    cos, sin = jnp.cos(ang), jnp.sin(ang)
    zero = jnp.zeros_like(sin)
    reps = LANES // d_rot
    cos_t = jnp.tile(jnp.concatenate([cos, cos], axis=-1), (1, reps))
    sa_t = jnp.tile(jnp.concatenate([-sin, zero], axis=-1), (1, reps))
    sb_t = jnp.tile(jnp.concatenate([zero, sin], axis=-1), (1, reps))
    pad = lambda a, v: jnp.concatenate([a, jnp.full((C, LANES), v, jnp.float32)], axis=0)
    return pad(cos_t, 1.0), pad(sa_t, 0.0), pad(sb_t, 0.0)


def _split_w_q_up(w):
    L, K, _ = w.shape
    w = w.reshape(L, K, N_HEADS, MLA_NOPE + MLA_ROPE)
    w = jnp.pad(w, ((0, 0), (0, 0), (0, 0), (0, MLA_QK - MLA_NOPE - MLA_ROPE)))
    return w.reshape(L, K, N_HEADS * MLA_QK).astype(jnp.bfloat16)


def _split_w_kv_up(w):
    L, K, _ = w.shape
    w = w.reshape(L, K, N_HEADS, 2, HEAD_DIM)
    return jnp.swapaxes(w, 2, 3).reshape(L, K, 2 * MIX_W).astype(jnp.bfloat16)


def _lane_row(v, reps=1, pad=0):
    r = jnp.tile(v.astype(jnp.float32), reps)
    return jnp.pad(r, (0, pad)).reshape(1, LANES)


def kernel(x, c, ctx, c_ctx, w_ada, b_ada, g_norm, w_in, qn_a, kn_a, sink_a, qn_b, kn_b, rpb_b, g_qa, g_kva,
           w_q_up, w_kv_up, qn_nope, qn_pe, kn_nope, kn_pe, qn_d, kn_d, lam_q1, lam_k1, lam_q2, lam_k2,
           subln_d, w_branch, w_out):
    B, S, D = x.shape
    C = ctx.shape[1]
    L = w_ada.shape[0]
    T = S + C
    R = B * T
    rows = S // GRID_W
    assert D == 4 * MIX_W and S % 1024 == 0 and C == ROW_TILE and B + 1 <= 8
    assert rows % NA_QROWS == 0 and rows >= NA_KROWS + NA_QROWS
    assert w_in.shape[-1] == P2_START + C2_MG + 4 * D

    xs = jnp.concatenate([x, ctx], axis=1)
    cc = jnp.zeros((8, D), jnp.float32).at[:B].set(c).at[B].set(c_ctx)
    mods4 = _ada(cc, w_ada, b_ada).reshape(L, 8, 1, 3 * D)

    w1 = _cast_range1(w_in)
    w2 = _cast_range2(w_in)
    w_q_p = _split_w_q_up(w_q_up)
    w_kv_p = _split_w_kv_up(w_kv_up)
    w_branch_p = w_branch.astype(jnp.bfloat16)
    w_out_p = w_out.astype(jnp.bfloat16)
    tab_a = _rope_tables(S, C, HEAD_DIM)
    tab_d = _rope_tables(S, C, DIFF_DIM)
    bias_b = _neighbour_bias(rpb_b, rows)

    for l in range(L):
        lambda_init = 0.8 - 0.6 * math.exp(-0.3 * l)
        h = _prologue(xs, g_norm[l], mods4, l, S).reshape(R, D)
        p1 = _in_proj(h, w1, l, "in_proj_1")
        p2 = _in_proj(h, w2, l, "in_proj_2")

        qa, ka = _normrope(p1, C1_AQ, MIX_W, C1_AK, WA_KV_HEADS * HEAD_DIM, _lane_row(qn_a[l]), _lane_row(kn_a[l]),
                           tab_a, B=B, T=T, group=HEAD_DIM, half=HEAD_DIM // 2, scale=HEAD_DIM ** -0.5,
                           name="prep_window")
        sink = jnp.zeros((WA_KV_HEADS, 8, LANES), jnp.float32).at[:, :WA_GROUP].set(
            jnp.broadcast_to(sink_a[l].astype(jnp.float32).reshape(WA_KV_HEADS, WA_GROUP, 1),
                             (WA_KV_HEADS, WA_GROUP, LANES)))
        ua = _window_attn(qa, ka, p1, sink, S=S, C=C)

        ub = _neighbour_attn(p1, _lane_row(qn_b[l]), _lane_row(kn_b[l]), bias_b, l,
                             B=B, S=S, C=C, scale=HEAD_DIM ** -0.5)

        qc, kc, vc = _mla_prep(p1, g_qa[l].reshape(1, -1), g_kva[l].reshape(1, -1), w_q_p, w_kv_p, l,
                               _lane_row(qn_nope[l]), _lane_row(qn_pe[l], pad=LANES - MLA_ROPE),
                               _lane_row(kn_nope[l]), _lane_row(kn_pe[l], pad=LANES - MLA_ROPE),
                               tab_d, B=B, T=T, scale=(MLA_NOPE + MLA_ROPE) ** -0.5 * LOG2E)
        uc = _mla_attn(qc, kc, vc, p2, S=S, C=C)

        qd, kd = _normrope(p2, C2_DQ, MIX_W, C2_DK, MIX_W, _lane_row(qn_d[l], reps=2), _lane_row(kn_d[l], reps=2),
                           tab_d, B=B, T=T, group=DIFF_DIM, half=DIFF_DIM // 2, scale=DIFF_DIM ** -0.5 * LOG2E,
                           name="prep_diff")
        lam_rows = jnp.zeros((8, LANES), jnp.float32).at[:4, :DIFF_DIM].set(
            jnp.stack([lam_q1[l], lam_k1[l], lam_q2[l], lam_k2[l]]).astype(jnp.float32))
        ud = _diff_attn(qd, kd, p2, lam_rows, _lane_row(subln_d[l]), S=S, C=C, lambda_init=lambda_init)

        merged = _merge((ua, ub, uc, ud), p2, w_branch_p, l)
        xs = _out_proj(merged, w_out_p, xs, mods4, l, S, S if l == L - 1 else T)
    return xs
```

```python
import functools
import math

import numpy as np
import jax
import jax.numpy as jnp
from jax import lax
from jax.experimental import pallas as pl
from jax.experimental.pallas import tpu as pltpu

GRID_W = 64
HEAD_DIM = 128
N_HEADS = 8
MIX_W = N_HEADS * HEAD_DIM
ROPE_BASE = 10000.0
NEG_INF = -1e30
EPS = 1e-6
LOG2E = math.log2(math.e)
WA_KV_HEADS = 2
WA_GROUP = N_HEADS // WA_KV_HEADS
WINDOW = 128
NA_KH = 8
NA_KW = 16
NA_QROWS = 4
NA_KROWS = NA_QROWS + NA_KH
MLA_Q_LORA = 896
MLA_KV_LORA = 512
MLA_NOPE = 128
MLA_ROPE = 64
MLA_QK = 256
DIFF_DIM = 64

LANES = 128
BF16_ROWS = 16
ROW_TILE = 256
VMEM_LIMIT = 56 * 1024 * 1024

P1_W = 8192
P2_START = 8128
C1_AQ, C1_AK, C1_AV, C1_AG = 0, 1024, 1280, 1536
C1_BQ, C1_BK, C1_BV, C1_BG = 2560, 3584, 4608, 5632
C1_MLA, MLA_IN_W = 6144, 2048
X_CQ, X_CKV, X_KPE = 512, 1408, 1920
C2_CG, C2_DQ, C2_DK, C2_DV, C2_DG, C2_MG = 0, 1024, 2048, 3072, 4096, 5120


def _cparams(*sem):
    return pltpu.CompilerParams(dimension_semantics=sem, vmem_limit_bytes=VMEM_LIMIT)


def _dot(a, b):
    return jnp.dot(a, b, preferred_element_type=jnp.float32)


def _dot_nt(a, b):
    return lax.dot_general(a, b, (((1,), (1,)), ((), ())), preferred_element_type=jnp.float32)


def _sigmoid(v):
    return 0.5 * jnp.tanh(0.5 * v) + 0.5


def _silu(v):
    return v * _sigmoid(v)


def _row_tile(n, cap):
    return max(t for t in range(BF16_ROWS, cap + 1, BF16_ROWS) if n % t == 0)


def _ada_kernel(c_ref, w_ref, b_ref, o_ref):
    s = _silu(c_ref[...]).astype(jnp.bfloat16)
    o_ref[...] = _dot(s, w_ref[...].astype(jnp.bfloat16)) + b_ref[...]


def _ada(cc, w_ada, b_ada):
    L, D, N = w_ada.shape
    tn = 512
    return pl.pallas_call(
        _ada_kernel,
        out_shape=jax.ShapeDtypeStruct((L, 8, N), jnp.float32),
        grid=(L, N // tn),
        in_specs=[
            pl.BlockSpec((8, D), lambda l, n: (0, 0)),
            pl.BlockSpec((None, D, tn), lambda l, n: (l, 0, n)),
            pl.BlockSpec((None, 1, tn), lambda l, n: (l, 0, n)),
        ],
        out_specs=pl.BlockSpec((None, 8, tn), lambda l, n: (l, 0, n)),
        compiler_params=_cparams("arbitrary", "arbitrary"),
        name="ada_mod",
    )(cc, w_ada, b_ada.reshape(L, 1, N))


def _prologue_kernel(x_ref, g_ref, sh_ref, sc_ref, o_ref):
    x = x_ref[...]
    ms = jnp.mean(x * x, axis=-1, keepdims=True)
    y = x * lax.rsqrt(ms + EPS) * g_ref[...]
    o_ref[...] = (y * (1.0 + sc_ref[...]) + sh_ref[...]).astype(o_ref.dtype)


def _mod_row(S, B):
    return lambda b, t: jnp.where(t >= S // ROW_TILE, B, b)


def _prologue(xs, g, mods4, l, S):
    B, T, D = xs.shape
    row = _mod_row(S, B)
    return pl.pallas_call(
        _prologue_kernel,
        out_shape=jax.ShapeDtypeStruct((B, T, D), jnp.bfloat16),
        grid=(B, T // ROW_TILE),
        in_specs=[
            pl.BlockSpec((None, ROW_TILE, D), lambda b, t: (b, t, 0)),
            pl.BlockSpec((1, D), lambda b, t: (0, 0)),
            pl.BlockSpec((None, None, 1, D), lambda b, t: (l, row(b, t), 0, 0)),
            pl.BlockSpec((None, None, 1, D), lambda b, t: (l, row(b, t), 0, 1)),
        ],
        out_specs=pl.BlockSpec((None, ROW_TILE, D), lambda b, t: (b, t, 0)),
        compiler_params=_cparams("arbitrary", "arbitrary"),
        name="norm_modulate",
    )(xs, g.reshape(1, D), mods4, mods4)


def _in_proj_f32_kernel(a_ref, w_ref, o_ref, wb_ref):
    @pl.when(pl.program_id(1) == 0)
    def _():
        wb_ref[...] = w_ref[...].astype(wb_ref.dtype)

    o_ref[...] = _dot_nt(a_ref[...], wb_ref[...]).astype(o_ref.dtype)


def _in_proj_bf16_kernel(a_ref, w_ref, o_ref):
    o_ref[...] = _dot_nt(a_ref[...], w_ref[...]).astype(o_ref.dtype)


def _in_proj(h, wt, l, n_cols, name):
    R, D = h.shape
    cast = wt.dtype != jnp.bfloat16
    tm, tn = _row_tile(R, 640), (512 if cast else 1024)
    return pl.pallas_call(
        _in_proj_f32_kernel if cast else _in_proj_bf16_kernel,
        out_shape=jax.ShapeDtypeStruct((R, n_cols), jnp.bfloat16),
        grid=(n_cols // tn, R // tm),
        in_specs=[
            pl.BlockSpec((tm, D), lambda n, m: (m, 0)),
            pl.BlockSpec((None, tn, D), lambda n, m: (l, n, 0)),
        ],
        out_specs=pl.BlockSpec((tm, tn), lambda n, m: (m, n)),
        scratch_shapes=[pltpu.VMEM((tn, D), jnp.bfloat16)] if cast else [],
        compiler_params=_cparams("arbitrary", "arbitrary"),
        name=name,
    )(h, wt)


def _rms128(x, gain, group, count=None):
    sq = x * x
    tot = jnp.sum(sq, axis=-1, keepdims=True)
    if group == LANES:
        inv = lax.rsqrt(tot / (count or group) + EPS)
    else:
        lane = lax.broadcasted_iota(jnp.int32, x.shape, 1)
        low = lane < group
        lo = jnp.sum(jnp.where(low, sq, 0.0), axis=-1, keepdims=True)
        inv = jnp.where(low, lax.rsqrt(lo / group + EPS), lax.rsqrt((tot - lo) / group + EPS))
    return x * inv * gain


def _rope128(x, cos, sa, sb, half):
    return x * cos + pltpu.roll(x, LANES - half, 1) * sa + pltpu.roll(x, half, 1) * sb


def _normrope_kernel(q_ref, k_ref, qg_ref, kg_ref, cos_ref, sa_ref, sb_ref, qo_ref, ko_ref,
                     *, group, half, scale):
    cos, sa, sb = cos_ref[...], sa_ref[...], sb_ref[...]
    for src, gain, dst, mul in ((q_ref, qg_ref, qo_ref, scale), (k_ref, kg_ref, ko_ref, None)):
        g = gain[...]
        for j in range(src.shape[-1] // LANES):
            sl = slice(j * LANES, (j + 1) * LANES)
            y = _rope128(_rms128(src[:, sl].astype(jnp.float32), g, group), cos, sa, sb, half)
            if mul is not None:
                y = y * mul
            dst[:, sl] = y.astype(dst.dtype)


def _normrope(p, col_q, wq, col_k, wk, q_gain, k_gain, tables, *, B, T, group, half, scale, name):
    nt = T // ROW_TILE
    tab_spec = pl.BlockSpec((ROW_TILE, LANES), lambda b, t: (t, 0))
    gain_spec = pl.BlockSpec((1, LANES), lambda b, t: (0, 0))
    return pl.pallas_call(
        functools.partial(_normrope_kernel, group=group, half=half, scale=scale),
        out_shape=(jax.ShapeDtypeStruct((B, T, wq), jnp.bfloat16),
                   jax.ShapeDtypeStruct((B, T, wk), jnp.bfloat16)),
        grid=(B, nt),
        in_specs=[
            pl.BlockSpec((ROW_TILE, wq), lambda b, t: (b * nt + t, col_q // wq)),
            pl.BlockSpec((ROW_TILE, wk), lambda b, t: (b * nt + t, col_k // wk)),
            gain_spec, gain_spec, tab_spec, tab_spec, tab_spec,
        ],
        out_specs=(pl.BlockSpec((None, ROW_TILE, wq), lambda b, t: (b, t, 0)),
                   pl.BlockSpec((None, ROW_TILE, wk), lambda b, t: (b, t, 0))),
        compiler_params=_cparams("arbitrary", "arbitrary"),
        name=name,
    )(p, p, q_gain, k_gain, *tables)


def _mla_prep_kernel(x_ref, gqa_ref, gkva_ref, wq_ref, wkv_ref,
                     qnn_ref, qnp_ref, knn_ref, knp_ref, cos_ref, sa_ref, sb_ref,
                     q_ref, k_ref, v_ref, *, scale):
    cos, sa, sb = cos_ref[...], sa_ref[...], sb_ref[...]
    half = MLA_ROPE // 2

    def full_rms(v, gain_ref):
        v = v.astype(jnp.float32)
        ms = jnp.mean(v * v, axis=-1, keepdims=True)
        return (v * lax.rsqrt(ms + EPS) * gain_ref[...]).astype(jnp.bfloat16)

    cq = x_ref[:, X_CQ:X_CQ + MLA_Q_LORA]
    ckv = x_ref[:, X_CKV:X_CKV + MLA_KV_LORA]
    kpe = x_ref[:, X_KPE:X_KPE + LANES].astype(jnp.float32)
    kpe = jnp.where(lax.broadcasted_iota(jnp.int32, kpe.shape, 1) < MLA_ROPE, kpe, 0.0)
    qh = _dot(full_rms(cq, gqa_ref), wq_ref[...])
    kv = _dot(full_rms(ckv, gkva_ref), wkv_ref[...])
    kpe = _rope128(_rms128(kpe, knp_ref[...], LANES, MLA_ROPE), cos, sa, sb, half).astype(k_ref.dtype)
    qnn, qnp, knn = qnn_ref[...], qnp_ref[...], knn_ref[...]
    for h in range(N_HEADS):
        nope = slice(h * MLA_QK, h * MLA_QK + LANES)
        pe = slice(h * MLA_QK + LANES, (h + 1) * MLA_QK)
        q_ref[:, nope] = (_rms128(qh[:, nope], qnn, LANES) * scale).astype(q_ref.dtype)
        qpe = _rms128(qh[:, pe], qnp, LANES, MLA_ROPE)
        q_ref[:, pe] = (_rope128(qpe, cos, sa, sb, half) * scale).astype(q_ref.dtype)
        k_ref[:, nope] = _rms128(kv[:, h * LANES:(h + 1) * LANES], knn, LANES).astype(k_ref.dtype)
        k_ref[:, pe] = kpe
    v_ref[...] = kv[:, MIX_W:].astype(v_ref.dtype)


def _mla_prep(p1, g_qa, g_kva, wq, wkv, l, qnn, qnp, knn, knp, tables, *, B, T, scale):
    nt = T // ROW_TILE
    const = lambda b, t: (0, 0)
    gain_spec = pl.BlockSpec((1, LANES), const)
    tab_spec = pl.BlockSpec((ROW_TILE, LANES), lambda b, t: (t, 0))
    out_qk = jax.ShapeDtypeStruct((B, T, N_HEADS * MLA_QK), jnp.bfloat16)
    spec_qk = pl.BlockSpec((None, ROW_TILE, N_HEADS * MLA_QK), lambda b, t: (b, t, 0))
    return pl.pallas_call(
        functools.partial(_mla_prep_kernel, scale=scale),
        out_shape=(out_qk, out_qk, jax.ShapeDtypeStruct((B, T, MIX_W), jnp.bfloat16)),
        grid=(B, nt),
        in_specs=[
            pl.BlockSpec((ROW_TILE, MLA_IN_W), lambda b, t: (b * nt + t, C1_MLA // MLA_IN_W)),
            pl.BlockSpec((1, MLA_Q_LORA), const),
            pl.BlockSpec((1, MLA_KV_LORA), const),
            pl.BlockSpec((None,) + wq.shape[1:], lambda b, t: (l, 0, 0)),
            pl.BlockSpec((None,) + wkv.shape[1:], lambda b, t: (l, 0, 0)),
            gain_spec, gain_spec, gain_spec, gain_spec, tab_spec, tab_spec, tab_spec,
        ],
        out_specs=(spec_qk, spec_qk, pl.BlockSpec((None, ROW_TILE, MIX_W), lambda b, t: (b, t, 0))),
        compiler_params=_cparams("arbitrary", "arbitrary"),
        name="mla_prep",
    )(p1, g_qa, g_kva, wq, wkv, qnn, qnp, knn, knp, *tables)


def _window_kernel(q_ref, k_ref, v_ref, g_ref, sink_ref, o_ref, *, S, C):
    blk = WINDOW
    band = 3 * blk
    t = pl.program_id(2)
    kc = k_ref[S:S + C, :]
    vc = v_ref[S:S + C, :]

    def stack(x):
        return jnp.concatenate([x[:, r * HEAD_DIM:(r + 1) * HEAD_DIM] for r in range(WA_GROUP)], axis=0)

    def sink_col(n):
        return jnp.concatenate([jnp.broadcast_to(sink_ref[r:r + 1, :1], (n, 1)) for r in range(WA_GROUP)], axis=0)

    def attend(parts, sink):
        m = sink
        for s, _ in parts:
            m = jnp.maximum(m, jnp.max(s, axis=-1, keepdims=True))
        den = jnp.exp(sink - m)
        acc = None
        for s, v in parts:
            e = jnp.exp(s - m)
            den = den + jnp.sum(e, axis=-1, keepdims=True)
            pv = _dot(e.astype(v.dtype), v)
            acc = pv if acc is None else acc + pv
        return acc / den

    def store(r0, n, o):
        for r in range(WA_GROUP):
            cols = slice(r * HEAD_DIM, (r + 1) * HEAD_DIM)
            g = g_ref[r0:r0 + n, cols].astype(jnp.float32)
            o_ref[r0:r0 + n, cols] = (o[r * n:(r + 1) * n] * _silu(g)).astype(o_ref.dtype)

    @pl.when(t < S // ROW_TILE)
    def _():
        sink = sink_col(blk)
        for i in range(ROW_TILE // blk):
            q0 = t * ROW_TILE + i * blk
            q = stack(q_ref[i * blk:(i + 1) * blk, :])
            k0 = pl.multiple_of(jnp.clip(q0 - blk, 0, S - band), blk)
            kb = k_ref[pl.ds(k0, band), :]
            vb = v_ref[pl.ds(k0, band), :]
            s_lat = _dot_nt(q, kb)
            q_pos = q0 + (lax.broadcasted_iota(jnp.int32, s_lat.shape, 0) & (blk - 1))
            k_pos = k0 + lax.broadcasted_iota(jnp.int32, s_lat.shape, 1)
            s_lat = jnp.where(jnp.abs(q_pos - k_pos) <= WINDOW, s_lat, NEG_INF)
            store(i * blk, blk, attend([(s_lat, vb), (_dot_nt(q, kc), vc)], sink))

    @pl.when(t >= S // ROW_TILE)
    def _():
        q = stack(q_ref[...])
        store(0, ROW_TILE, attend([(_dot_nt(q, kc), vc)], sink_col(ROW_TILE)))


def _window_attn(q, k, p1, sink, *, S, C):
    B, T, _ = q.shape
    nt = T // ROW_TILE
    gw = WA_GROUP * HEAD_DIM
    return pl.pallas_call(
        functools.partial(_window_kernel, S=S, C=C),
        out_shape=jax.ShapeDtypeStruct((B, T, MIX_W), jnp.bfloat16),
        grid=(B, WA_KV_HEADS, nt),
        in_specs=[
            pl.BlockSpec((None, ROW_TILE, gw), lambda b, g, t: (b, t, g)),
            pl.BlockSpec((None, T, HEAD_DIM), lambda b, g, t: (b, 0, g)),
            pl.BlockSpec((T, HEAD_DIM), lambda b, g, t: (b, C1_AV // HEAD_DIM + g)),
            pl.BlockSpec((ROW_TILE, gw), lambda b, g, t: (b * nt + t, C1_AG // gw + g)),
            pl.BlockSpec((None, 8, LANES), lambda b, g, t: (g, 0, 0)),
        ],
        out_specs=pl.BlockSpec((None, ROW_TILE, gw), lambda b, g, t: (b, t, g)),
        compiler_params=_cparams("arbitrary", "arbitrary", "arbitrary"),
        name="window_attn",
    )(q, k, p1, p1, sink)


NB_HEADS = 2


def _neighbour_kernel(q_ref, k_ref, v_ref, g_ref, qg_ref, kg_ref, bias_ref, o_ref, kn_ref, *, S, C, scale):
    rows = S // GRID_W
    n_blk = rows // NA_QROWS
    kn = NA_KROWS * GRID_W
    T = S + C
    t = pl.program_id(2)

    @pl.when(t == 0)
    def _():
        def norm_body(i, carry):
            r0 = pl.multiple_of(i * ROW_TILE, ROW_TILE)
            for hh in range(NB_HEADS):
                cols = slice(hh * HEAD_DIM, (hh + 1) * HEAD_DIM)
                k = _rms128(k_ref[pl.ds(r0, ROW_TILE), cols].astype(jnp.float32), kg_ref[...], LANES)
                kn_ref[pl.ds(r0, ROW_TILE), cols] = k.astype(kn_ref.dtype)
            return carry

        lax.fori_loop(0, T // ROW_TILE, norm_body, 0)

    def softmax_pv(parts):
        m = None
        for s, _ in parts:
            mx = jnp.max(s, axis=-1, keepdims=True)
            m = mx if m is None else jnp.maximum(m, mx)
        den, acc = None, None
        for s, v in parts:
            e = jnp.exp(s - m)
            sm = jnp.sum(e, axis=-1, keepdims=True)
            pv = _dot(e.astype(v.dtype), v)
            den = sm if den is None else den + sm
            acc = pv if acc is None else acc + pv
        return acc / den

    def head(hh, latent):
        cols = slice(hh * HEAD_DIM, (hh + 1) * HEAD_DIM)
        q = (_rms128(q_ref[:, cols].astype(jnp.float32), qg_ref[...], LANES) * scale).astype(kn_ref.dtype)
        parts = [(_dot_nt(q, kn_ref[S:S + C, cols]), v_ref[S:S + C, cols])]
        if latent:
            ws = jnp.clip(t * NA_QROWS - NA_KH // 2, 0, rows - NA_KROWS)
            k0 = pl.multiple_of(ws * GRID_W, GRID_W)
            variant = jnp.where(t == 0, 0, jnp.where(t == n_blk - 1, 2, 1))
            s_lat = _dot_nt(q, kn_ref[pl.ds(k0, kn), cols]) + bias_ref[variant, hh]
            parts.insert(0, (s_lat, v_ref[pl.ds(k0, kn), cols]))
        g = g_ref[:, cols].astype(jnp.float32)
        o_ref[:, cols] = (softmax_pv(parts) * _silu(g)).astype(o_ref.dtype)

    @pl.when(t < n_blk)
    def _():
        for hh in range(NB_HEADS):
            head(hh, True)

    @pl.when(t >= n_blk)
    def _():
        for hh in range(NB_HEADS):
            head(hh, False)


def _neighbour_attn(p1, q_gain, k_gain, bias, l, *, B, S, C, scale):
    T = S + C
    nt = T // ROW_TILE
    bw = NB_HEADS * HEAD_DIM
    assert NA_QROWS * GRID_W == ROW_TILE
    gain_spec = pl.BlockSpec((1, LANES), lambda b, h, t: (0, 0))
    tile = lambda c0: pl.BlockSpec((ROW_TILE, bw), lambda b, h, t: (b * nt + t, c0 // bw + h))
    full = lambda c0: pl.BlockSpec((T, bw), lambda b, h, t: (b, c0 // bw + h))
    return pl.pallas_call(
        functools.partial(_neighbour_kernel, S=S, C=C, scale=scale),
        out_shape=jax.ShapeDtypeStruct((B, T, MIX_W), jnp.bfloat16),
        grid=(B, N_HEADS // NB_HEADS, nt),
        in_specs=[
            tile(C1_BQ), full(C1_BK), full(C1_BV), tile(C1_BG), gain_spec, gain_spec,
            pl.BlockSpec((None, 3, NB_HEADS) + bias.shape[3:], lambda b, h, t: (l, 0, h, 0, 0)),
        ],
        out_specs=pl.BlockSpec((None, ROW_TILE, bw), lambda b, h, t: (b, t, h)),
        scratch_shapes=[pltpu.VMEM((T, bw), jnp.bfloat16)],
        compiler_params=_cparams("arbitrary", "arbitrary", "arbitrary"),
        name="neighbour_attn",
    )(p1, p1, p1, p1, q_gain, k_gain, bias)


def _neighbour_bias(rpb, rows):
    W = GRID_W
    a = np.arange(NA_QROWS)
    j = np.arange(NA_KROWS)
    w = np.arange(W)
    cs = np.clip(w - NA_KW // 2, 0, W - NA_KW)
    col_ok = (w[None, :] >= cs[:, None]) & (w[None, :] < cs[:, None] + NA_KW)
    dc = np.clip(w[None, :] - w[:, None], -(NA_KW - 1), NA_KW - 1) + (NA_KW - 1)
    pick_dc = (dc[None] == np.arange(2 * NA_KW - 1)[:, None, None]).astype(np.float32)
    dr_all, ok_all = [], []
    for r0 in (0, NA_QROWS, rows - NA_QROWS):
        ws = int(np.clip(r0 - NA_KH // 2, 0, rows - NA_KROWS))
        rq = r0 + a
        kr = ws + j
        rs = np.clip(rq - NA_KH // 2, 0, rows - NA_KH)
        row_ok = (kr[None, :] >= rs[:, None]) & (kr[None, :] < rs[:, None] + NA_KH)
        dr_all.append(np.clip(kr[None, :] - rq[:, None] + (NA_KH - 1), 0, 2 * NA_KH - 2))
        ok_all.append(row_ok[:, None, :, None] & col_ok[None, :, None, :])
    dr = np.stack(dr_all)
    ok = np.stack(ok_all)
    rows_sel = rpb.astype(jnp.float32)[:, :, dr, :]
    val = jnp.einsum("lhvajd,dqk->lvhaqjk", rows_sel, pick_dc, precision=lax.Precision.HIGHEST)
    val = jnp.where(ok[None, :, None], val, NEG_INF)
    L, H = rpb.shape[:2]
    return val.reshape(L, 3, H, NA_QROWS * W, NA_KROWS * W)


def _flash_tile(q, k_ref, v_ref, s_ref, acc_ref, *, S, C, chunk, with_latent):
    rows = q.shape[0]
    acc = acc_ref.at[:rows]

    def scores(k0, n):
        return _dot_nt(q, k_ref[pl.ds(k0, n), :])

    def update(s, k0, n, m):
        vals = v_ref[pl.ds(k0, n), :]
        vals = jnp.concatenate([vals, jnp.ones_like(vals)], axis=1)
        mx = jnp.max(s, axis=-1, keepdims=True)
        if m is None:
            m_new = mx
            acc[...] = _dot(jnp.exp2(s - m_new).astype(vals.dtype), vals)
        else:
            m_new = jnp.maximum(m, mx)
            acc[...] = jnp.exp2(m - m_new) * acc[...] + _dot(jnp.exp2(s - m_new).astype(vals.dtype), vals)
        return m_new

    if with_latent:
        n_pairs = S // (2 * chunk)
        s_even, s_odd = s_ref.at[0, :rows], s_ref.at[1, :rows]
        s_even[...] = scores(0, chunk)
        m = update(scores(S, C), S, C, None)

        def pair(k0, m, last):
            s_odd[...] = scores(k0 + chunk, chunk)
            m = update(s_even[...], k0, chunk, m)
            if not last:
                s_even[...] = scores(k0 + 2 * chunk, chunk)
            return update(s_odd[...], k0 + chunk, chunk, m)

        m = lax.fori_loop(0, n_pairs - 1,
                          lambda i, m: pair(pl.multiple_of(i * 2 * chunk, 2 * chunk), m, False), m)
        pair((n_pairs - 1) * 2 * chunk, m, True)
    else:
        update(scores(S, C), S, C, None)
    return acc[:, :HEAD_DIM] / acc[:, HEAD_DIM:]


def _mla_kernel(q_ref, k_ref, v_ref, g_ref, o_ref, s_ref, acc_ref, *, S, C, tq, chunk):
    def tile(r0, n, with_latent):
        o = _flash_tile(q_ref[pl.ds(r0, n), :], k_ref, v_ref, s_ref, acc_ref,
                        S=S, C=C, chunk=chunk, with_latent=with_latent)
        g = g_ref[pl.ds(r0, n), :].astype(jnp.float32)
        o_ref[pl.ds(r0, n), :] = (o * _silu(g)).astype(o_ref.dtype)

    def body(i, carry):
        tile(pl.multiple_of(i * tq, tq), tq, True)
        return carry

    lax.fori_loop(0, S // tq, body, 0)
    tile(S, C, False)


def _flash_scratch(rows, chunk):
    return [pltpu.VMEM((2, rows, chunk), jnp.float32), pltpu.VMEM((rows, 2 * HEAD_DIM), jnp.float32)]


def _mla_attn(q, k, v, p2, *, S, C):
    B, T, _ = q.shape
    head = lambda b, h: (b, 0, h)
    tq, chunk = 1024, min(1024, S // 2)
    return pl.pallas_call(
        functools.partial(_mla_kernel, S=S, C=C, tq=tq, chunk=chunk),
        scratch_shapes=_flash_scratch(tq, chunk),
        out_shape=jax.ShapeDtypeStruct((B, T, MIX_W), jnp.bfloat16),
        grid=(B, N_HEADS),
        in_specs=[
            pl.BlockSpec((None, T, MLA_QK), head),
            pl.BlockSpec((None, T, MLA_QK), head),
            pl.BlockSpec((None, T, HEAD_DIM), head),
            pl.BlockSpec((T, HEAD_DIM), lambda b, h: (b, C2_CG // HEAD_DIM + h)),
        ],
        out_specs=pl.BlockSpec((None, T, HEAD_DIM), head),
        compiler_params=_cparams("arbitrary", "arbitrary"),
        name="mla_attn",
    )(q, k, v, p2)


def _diff_kernel(q_ref, k_ref, v_ref, g_ref, lam_ref, sub_ref, o_ref, s_ref, acc_ref,
                 *, S, C, tq, chunk, lambda_init):
    lv = lam_ref[...]
    lam = (jnp.exp(jnp.sum(lv[0:1] * lv[1:2], axis=-1, keepdims=True))
           - jnp.exp(jnp.sum(lv[2:3] * lv[3:4], axis=-1, keepdims=True)) + lambda_init)
    sub = sub_ref[...] * (1.0 - lambda_init)

    def tile(r0, n, with_latent):
        q = q_ref[pl.ds(r0, n), :]
        low = lax.broadcasted_iota(jnp.int32, q.shape, 1) < DIFF_DIM
        zero = jnp.zeros_like(q)
        q2 = jnp.concatenate([jnp.where(low, q, zero), jnp.where(low, zero, q)], axis=0)
        o = _flash_tile(q2, k_ref, v_ref, s_ref, acc_ref, S=S, C=C, chunk=chunk, with_latent=with_latent)
        d = o[:n] - lam * o[n:]
        ms = jnp.mean(d * d, axis=-1, keepdims=True)
        g = g_ref[pl.ds(r0, n), :].astype(jnp.float32)
        o_ref[pl.ds(r0, n), :] = (d * lax.rsqrt(ms + EPS) * sub * _silu(g)).astype(o_ref.dtype)

    def body(i, carry):
        tile(pl.multiple_of(i * tq, tq), tq, True)
        return carry

    lax.fori_loop(0, S // tq, body, 0)
    tile(S, C, False)


def _diff_attn(q, k, p2, lam_rows, subln, *, S, C, lambda_init):
    B, T, _ = q.shape
    head = lambda b, h: (b, 0, h)
    const = lambda b, h: (0, 0)
    tq, chunk = 512, min(1024, S // 2)
    return pl.pallas_call(
        functools.partial(_diff_kernel, S=S, C=C, tq=tq, chunk=chunk, lambda_init=lambda_init),
        scratch_shapes=_flash_scratch(2 * tq, chunk),
        out_shape=jax.ShapeDtypeStruct((B, T, MIX_W), jnp.bfloat16),
        grid=(B, N_HEADS),
        in_specs=[
            pl.BlockSpec((None, T, HEAD_DIM), head),
            pl.BlockSpec((None, T, HEAD_DIM), head),
            pl.BlockSpec((T, HEAD_DIM), lambda b, h: (b, C2_DV // HEAD_DIM + h)),
            pl.BlockSpec((T, HEAD_DIM), lambda b, h: (b, C2_DG // HEAD_DIM + h)),
            pl.BlockSpec((8, LANES), const),
            pl.BlockSpec((1, LANES), const),
        ],
        out_specs=pl.BlockSpec((None, T, HEAD_DIM), head),
        compiler_params=_cparams("arbitrary", "arbitrary"),
        name="diff_attn",
    )(q, k, p2, p2, lam_rows, subln)


def _merge_kernel(ua_ref, ub_ref, uc_ref, ud_ref, m0_ref, m1_ref, m2_ref, m3_ref, w_ref, o_ref):
    acc = None
    for i, (u, mg) in enumerate(zip((ua_ref, ub_ref, uc_ref, ud_ref), (m0_ref, m1_ref, m2_ref, m3_ref))):
        t = _sigmoid(mg[...].astype(jnp.float32)) * _dot(u[...], w_ref[i])
        acc = t if acc is None else acc + t
    o_ref[...] = acc.astype(o_ref.dtype)


def _merge(us, p2, wb, l):
    R = p2.shape[0]
    D = wb.shape[-1]
    tm, tn = _row_tile(R, 640), 1024
    u_spec = pl.BlockSpec((tm, MIX_W), lambda m, n: (m, 0))
    mg_specs = [pl.BlockSpec((tm, tn), functools.partial(lambda m, n, i: (m, (C2_MG + i * D) // tn + n), i=i))
                for i in range(4)]
    return pl.pallas_call(
        _merge_kernel,
        out_shape=jax.ShapeDtypeStruct((R, D), jnp.bfloat16),
        grid=(R // tm, D // tn),
        in_specs=[u_spec, u_spec, u_spec, u_spec, *mg_specs,
                  pl.BlockSpec((None, 4, MIX_W, tn), lambda m, n: (l, 0, 0, n))],
        out_specs=pl.BlockSpec((tm, tn), lambda m, n: (m, n)),
        compiler_params=_cparams("arbitrary", "arbitrary"),
        name="merge",
    )(*[u.reshape(R, MIX_W) for u in us], p2, p2, p2, p2, wb)


def _out_kernel(a_ref, w_ref, x_ref, gl_ref, gc_ref, o_ref, *, S):
    tm = o_ref.shape[0]
    row = pl.program_id(2) * tm + lax.broadcasted_iota(jnp.int32, o_ref.shape, 0)
    gate = jnp.where(row < S, gl_ref[...], gc_ref[...])
    o_ref[...] = x_ref[...] + gate * _dot(a_ref[...], w_ref[...])


def _out_proj(merged, w, xs, mods4, l, S, out_rows):
    B, T, D = xs.shape
    tm, tn = _row_tile(out_rows, 640), 1024
    gate_col = lambda n: 2 * (D // tn) + n
    return pl.pallas_call(
        functools.partial(_out_kernel, S=S),
        out_shape=jax.ShapeDtypeStruct((B, out_rows, D), jnp.float32),
        grid=(D // tn, B, out_rows // tm),
        in_specs=[
            pl.BlockSpec((None, tm, D), lambda n, b, t: (b, t, 0)),
            pl.BlockSpec((None, D, tn), lambda n, b, t: (l, 0, n)),
            pl.BlockSpec((None, tm, tn), lambda n, b, t: (b, t, n)),
            pl.BlockSpec((None, None, 1, tn), lambda n, b, t: (l, b, 0, gate_col(n))),
            pl.BlockSpec((None, None, 1, tn), lambda n, b, t: (l, B, 0, gate_col(n))),
        ],
        out_specs=pl.BlockSpec((None, tm, tn), lambda n, b, t: (b, t, n)),
        compiler_params=_cparams("arbitrary", "arbitrary", "arbitrary"),
        name="out_proj",
    )(merged.reshape(B, T, D), w, xs, mods4, mods4)


def _rope_tables(S, C, d_rot):
    t = jnp.arange(S, dtype=jnp.int32)
    row = (t // GRID_W).astype(jnp.float32)
    col = (t % GRID_W).astype(jnp.float32)
    n_f = d_rot // 4
    inv = jnp.power(ROPE_BASE, -jnp.arange(n_f, dtype=jnp.float32) / n_f)
    ang = jnp.concatenate([row[:, None] * inv, col[:, None] * inv], axis=-1)
    cos, sin = jnp.cos(ang), jnp.sin(ang)
    zero = jnp.zeros_like(sin)
    reps = LANES // d_rot
    cos_t = jnp.tile(jnp.concatenate([cos, cos], axis=-1), (1, reps))
    sa_t = jnp.tile(jnp.concatenate([-sin, zero], axis=-1), (1, reps))
    sb_t = jnp.tile(jnp.concatenate([zero, sin], axis=-1), (1, reps))
    pad = lambda a, v: jnp.concatenate([a, jnp.full((C, LANES), v, jnp.float32)], axis=0)
    return pad(cos_t, 1.0), pad(sa_t, 0.0), pad(sb_t, 0.0)


def _split_w_q_up(w):
    L, K, _ = w.shape
    w = w.reshape(L, K, N_HEADS, MLA_NOPE + MLA_ROPE)
    w = jnp.pad(w, ((0, 0), (0, 0), (0, 0), (0, MLA_QK - MLA_NOPE - MLA_ROPE)))
    return w.reshape(L, K, N_HEADS * MLA_QK).astype(jnp.bfloat16)


def _split_w_kv_up(w):
    L, K, _ = w.shape
    w = w.reshape(L, K, N_HEADS, 2, HEAD_DIM)
    return jnp.swapaxes(w, 2, 3).reshape(L, K, 2 * MIX_W).astype(jnp.bfloat16)


def _lane_row(v, reps=1, pad=0):
    r = jnp.tile(v.astype(jnp.float32), reps)
    return jnp.pad(r, (0, pad)).reshape(1, LANES)


def kernel(x, c, ctx, c_ctx, w_ada, b_ada, g_norm, w_in, qn_a, kn_a, sink_a, qn_b, kn_b, rpb_b, g_qa, g_kva,
           w_q_up, w_kv_up, qn_nope, qn_pe, kn_nope, kn_pe, qn_d, kn_d, lam_q1, lam_k1, lam_q2, lam_k2,
           subln_d, w_branch, w_out):
    B, S, D = x.shape
    C = ctx.shape[1]
    L = w_ada.shape[0]
    T = S + C
    R = B * T
    rows = S // GRID_W
    assert D == 4 * MIX_W and S % 1024 == 0 and C == ROW_TILE and B + 1 <= 8
    assert rows % NA_QROWS == 0 and rows >= NA_KROWS + NA_QROWS
    assert w_in.shape[-1] == P2_START + C2_MG + 4 * D

    xs = jnp.concatenate([x, ctx], axis=1)
    cc = jnp.zeros((8, D), jnp.float32).at[:B].set(c).at[B].set(c_ctx)
    mods4 = _ada(cc, w_ada, b_ada).reshape(L, 8, 1, 3 * D)

    w_in_t = jnp.swapaxes(w_in, 1, 2)
    w2_t = w_in_t[:, P2_START:].astype(jnp.bfloat16)
    w_q_p = _split_w_q_up(w_q_up)
    w_kv_p = _split_w_kv_up(w_kv_up)
    w_branch_p = w_branch.astype(jnp.bfloat16)
    w_out_p = w_out.astype(jnp.bfloat16)
    tab_a = _rope_tables(S, C, HEAD_DIM)
    tab_d = _rope_tables(S, C, DIFF_DIM)
    bias_b = _neighbour_bias(rpb_b, rows)

    for l in range(L):
        lambda_init = 0.8 - 0.6 * math.exp(-0.3 * l)
        h = _prologue(xs, g_norm[l], mods4, l, S).reshape(R, D)
        p1 = _in_proj(h, w_in_t, l, P1_W, "in_proj_1")
        p2 = _in_proj(h, w2_t, l, w2_t.shape[1], "in_proj_2")

        qa, ka = _normrope(p1, C1_AQ, MIX_W, C1_AK, WA_KV_HEADS * HEAD_DIM, _lane_row(qn_a[l]), _lane_row(kn_a[l]),
                           tab_a, B=B, T=T, group=HEAD_DIM, half=HEAD_DIM // 2, scale=HEAD_DIM ** -0.5,
                           name="prep_window")
        sink = jnp.zeros((WA_KV_HEADS, 8, LANES), jnp.float32).at[:, :WA_GROUP].set(
            jnp.broadcast_to(sink_a[l].astype(jnp.float32).reshape(WA_KV_HEADS, WA_GROUP, 1),
                             (WA_KV_HEADS, WA_GROUP, LANES)))
        ua = _window_attn(qa, ka, p1, sink, S=S, C=C)

        ub = _neighbour_attn(p1, _lane_row(qn_b[l]), _lane_row(kn_b[l]), bias_b, l,
                             B=B, S=S, C=C, scale=HEAD_DIM ** -0.5)

        qc, kc, vc = _mla_prep(p1, g_qa[l].reshape(1, -1), g_kva[l].reshape(1, -1), w_q_p, w_kv_p, l,
                               _lane_row(qn_nope[l]), _lane_row(qn_pe[l], pad=LANES - MLA_ROPE),
                               _lane_row(kn_nope[l]), _lane_row(kn_pe[l], pad=LANES - MLA_ROPE),
                               tab_d, B=B, T=T, scale=(MLA_NOPE + MLA_ROPE) ** -0.5 * LOG2E)
        uc = _mla_attn(qc, kc, vc, p2, S=S, C=C)

        qd, kd = _normrope(p2, C2_DQ, MIX_W, C2_DK, MIX_W, _lane_row(qn_d[l], reps=2), _lane_row(kn_d[l], reps=2),
                           tab_d, B=B, T=T, group=DIFF_DIM, half=DIFF_DIM // 2, scale=DIFF_DIM ** -0.5 * LOG2E,
                           name="prep_diff")
        lam_rows = jnp.zeros((8, LANES), jnp.float32).at[:4, :DIFF_DIM].set(
            jnp.stack([lam_q1[l], lam_k1[l], lam_q2[l], lam_k2[l]]).astype(jnp.float32))
        ud = _diff_attn(qd, kd, p2, lam_rows, _lane_row(subln_d[l]), S=S, C=C, lambda_init=lambda_init)

        merged = _merge((ua, ub, uc, ud), p2, w_branch_p, l)
        xs = _out_proj(merged, w_out_p, xs, mods4, l, S, S if l == L - 1 else T)
    return xs
```

```python
import functools
import math

import numpy as np
import jax
import jax.numpy as jnp
from jax import lax
from jax.experimental import pallas as pl
from jax.experimental.pallas import tpu as pltpu

GRID_W = 64
HEAD_DIM = 128
N_HEADS = 8
MIX_W = N_HEADS * HEAD_DIM
ROPE_BASE = 10000.0
NEG_INF = -1e30
EPS = 1e-6
LOG2E = math.log2(math.e)
WA_KV_HEADS = 2
WA_GROUP = N_HEADS // WA_KV_HEADS
WINDOW = 128
NA_KH = 8
NA_KW = 16
NA_QROWS = 4
NA_KROWS = NA_QROWS + NA_KH
MLA_Q_LORA = 896
MLA_KV_LORA = 512
MLA_NOPE = 128
MLA_ROPE = 64
MLA_QK = 256
DIFF_DIM = 64

LANES = 128
BF16_ROWS = 16
ROW_TILE = 256
VMEM_LIMIT = 56 * 1024 * 1024

P1_W = 8192
P2_START = 8128
C1_AQ, C1_AK, C1_AV, C1_AG = 0, 1024, 1280, 1536
C1_BQ, C1_BK, C1_BV, C1_BG = 2560, 3584, 4608, 5632
C1_MLA, MLA_IN_W = 6144, 2048
X_CQ, X_CKV, X_KPE = 512, 1408, 1920
C2_CG, C2_DQ, C2_DK, C2_DV, C2_DG, C2_MG = 0, 1024, 2048, 3072, 4096, 5120


def _cparams(*sem):
    return pltpu.CompilerParams(dimension_semantics=sem, vmem_limit_bytes=VMEM_LIMIT)


def _dot(a, b):
    return jnp.dot(a, b, preferred_element_type=jnp.float32)


def _dot_nt(a, b):
    return lax.dot_general(a, b, (((1,), (1,)), ((), ())), preferred_element_type=jnp.float32)


def _sigmoid(v):
    return 0.5 * jnp.tanh(0.5 * v) + 0.5


def _silu(v):
    return v * _sigmoid(v)


def _row_tile(n, cap):
    return max(t for t in range(BF16_ROWS, cap + 1, BF16_ROWS) if n % t == 0)


def _ada_kernel(c_ref, w_ref, b_ref, o_ref):
    s = _silu(c_ref[...]).astype(jnp.bfloat16)
    o_ref[...] = _dot(s, w_ref[...].astype(jnp.bfloat16)) + b_ref[...]


def _ada(cc, w_ada, b_ada):
    L, D, N = w_ada.shape
    tn = 512
    return pl.pallas_call(
        _ada_kernel,
        out_shape=jax.ShapeDtypeStruct((L, 8, N), jnp.float32),
        grid=(L, N // tn),
        in_specs=[
            pl.BlockSpec((8, D), lambda l, n: (0, 0)),
            pl.BlockSpec((None, D, tn), lambda l, n: (l, 0, n)),
            pl.BlockSpec((None, 1, tn), lambda l, n: (l, 0, n)),
        ],
        out_specs=pl.BlockSpec((None, 8, tn), lambda l, n: (l, 0, n)),
        compiler_params=_cparams("arbitrary", "arbitrary"),
        name="ada_mod",
    )(cc, w_ada, b_ada.reshape(L, 1, N))


def _prologue_kernel(x_ref, g_ref, sh_ref, sc_ref, o_ref):
    x = x_ref[...]
    ms = jnp.mean(x * x, axis=-1, keepdims=True)
    y = x * lax.rsqrt(ms + EPS) * g_ref[...]
    o_ref[...] = (y * (1.0 + sc_ref[...]) + sh_ref[...]).astype(o_ref.dtype)


def _mod_row(S, B):
    return lambda b, t: jnp.where(t >= S // ROW_TILE, B, b)


def _prologue(xs, g, mods4, l, S):
    B, T, D = xs.shape
    row = _mod_row(S, B)
    return pl.pallas_call(
        _prologue_kernel,
        out_shape=jax.ShapeDtypeStruct((B, T, D), jnp.bfloat16),
        grid=(B, T // ROW_TILE),
        in_specs=[
            pl.BlockSpec((None, ROW_TILE, D), lambda b, t: (b, t, 0)),
            pl.BlockSpec((1, D), lambda b, t: (0, 0)),
            pl.BlockSpec((None, None, 1, D), lambda b, t: (l, row(b, t), 0, 0)),
            pl.BlockSpec((None, None, 1, D), lambda b, t: (l, row(b, t), 0, 1)),
        ],
        out_specs=pl.BlockSpec((None, ROW_TILE, D), lambda b, t: (b, t, 0)),
        compiler_params=_cparams("arbitrary", "arbitrary"),
        name="norm_modulate",
    )(xs, g.reshape(1, D), mods4, mods4)


def _in_proj_kernel(a_ref, w_ref, o_ref):
    o_ref[...] = _dot_nt(a_ref[...], w_ref[...]).astype(o_ref.dtype)


def _in_proj(h, wt, l, name):
    R, D = h.shape
    n_cols = wt.shape[1]
    tm, tn = _row_tile(R, 1100), 1024
    return pl.pallas_call(
        _in_proj_kernel,
        out_shape=jax.ShapeDtypeStruct((R, n_cols), jnp.bfloat16),
        grid=(n_cols // tn, R // tm),
        in_specs=[
            pl.BlockSpec((tm, D), lambda n, m: (m, 0)),
            pl.BlockSpec((None, tn, D), lambda n, m: (l, n, 0)),
        ],
        out_specs=pl.BlockSpec((tm, tn), lambda n, m: (m, n)),
        compiler_params=_cparams("arbitrary", "arbitrary"),
        name=name,
    )(h, wt)


def _rms128(x, gain, group, count=None):
    sq = x * x
    tot = jnp.sum(sq, axis=-1, keepdims=True)
    if group == LANES:
        inv = lax.rsqrt(tot / (count or group) + EPS)
    else:
        lane = lax.broadcasted_iota(jnp.int32, x.shape, 1)
        low = lane < group
        lo = jnp.sum(jnp.where(low, sq, 0.0), axis=-1, keepdims=True)
        inv = jnp.where(low, lax.rsqrt(lo / group + EPS), lax.rsqrt((tot - lo) / group + EPS))
    return x * inv * gain


def _rope128(x, cos, sa, sb, half):
    return x * cos + pltpu.roll(x, LANES - half, 1) * sa + pltpu.roll(x, half, 1) * sb


def _normrope_kernel(q_ref, k_ref, qg_ref, kg_ref, cos_ref, sa_ref, sb_ref, qo_ref, ko_ref,
                     *, group, half, scale):
    cos, sa, sb = cos_ref[...], sa_ref[...], sb_ref[...]
    for src, gain, dst, mul in ((q_ref, qg_ref, qo_ref, scale), (k_ref, kg_ref, ko_ref, None)):
        g = gain[...]
        for j in range(src.shape[-1] // LANES):
            sl = slice(j * LANES, (j + 1) * LANES)
            y = _rope128(_rms128(src[:, sl].astype(jnp.float32), g, group), cos, sa, sb, half)
            if mul is not None:
                y = y * mul
            dst[:, sl] = y.astype(dst.dtype)


def _normrope(p, col_q, wq, col_k, wk, q_gain, k_gain, tables, *, B, T, group, half, scale, name):
    nt = T // ROW_TILE
    tab_spec = pl.BlockSpec((ROW_TILE, LANES), lambda b, t: (t, 0))
    gain_spec = pl.BlockSpec((1, LANES), lambda b, t: (0, 0))
    return pl.pallas_call(
        functools.partial(_normrope_kernel, group=group, half=half, scale=scale),
        out_shape=(jax.ShapeDtypeStruct((B, T, wq), jnp.bfloat16),
                   jax.ShapeDtypeStruct((B, T, wk), jnp.bfloat16)),
        grid=(B, nt),
        in_specs=[
            pl.BlockSpec((ROW_TILE, wq), lambda b, t: (b * nt + t, col_q // wq)),
            pl.BlockSpec((ROW_TILE, wk), lambda b, t: (b * nt + t, col_k // wk)),
            gain_spec, gain_spec, tab_spec, tab_spec, tab_spec,
        ],
        out_specs=(pl.BlockSpec((None, ROW_TILE, wq), lambda b, t: (b, t, 0)),
                   pl.BlockSpec((None, ROW_TILE, wk), lambda b, t: (b, t, 0))),
        compiler_params=_cparams("arbitrary", "arbitrary"),
        name=name,
    )(p, p, q_gain, k_gain, *tables)


def _mla_prep_kernel(x_ref, gqa_ref, gkva_ref, wq_ref, wkv_ref,
                     qnn_ref, qnp_ref, knn_ref, knp_ref, cos_ref, sa_ref, sb_ref,
                     q_ref, k_ref, v_ref, *, scale):
    cos, sa, sb = cos_ref[...], sa_ref[...], sb_ref[...]
    half = MLA_ROPE // 2

    def full_rms(v, gain_ref):
        v = v.astype(jnp.float32)
        ms = jnp.mean(v * v, axis=-1, keepdims=True)
        return (v * lax.rsqrt(ms + EPS) * gain_ref[...]).astype(jnp.bfloat16)

    cq = x_ref[:, X_CQ:X_CQ + MLA_Q_LORA]
    ckv = x_ref[:, X_CKV:X_CKV + MLA_KV_LORA]
    kpe = x_ref[:, X_KPE:X_KPE + LANES].astype(jnp.float32)
    kpe = jnp.where(lax.broadcasted_iota(jnp.int32, kpe.shape, 1) < MLA_ROPE, kpe, 0.0)
    qh = _dot(full_rms(cq, gqa_ref), wq_ref[...])
    kv = _dot(full_rms(ckv, gkva_ref), wkv_ref[...])
    kpe = _rope128(_rms128(kpe, knp_ref[...], LANES, MLA_ROPE), cos, sa, sb, half).astype(k_ref.dtype)
    qnn, qnp, knn = qnn_ref[...], qnp_ref[...], knn_ref[...]
    for h in range(N_HEADS):
        nope = slice(h * MLA_QK, h * MLA_QK + LANES)
        pe = slice(h * MLA_QK + LANES, (h + 1) * MLA_QK)
        q_ref[:, nope] = (_rms128(qh[:, nope], qnn, LANES) * scale).astype(q_ref.dtype)
        qpe = _rms128(qh[:, pe], qnp, LANES, MLA_ROPE)
        q_ref[:, pe] = (_rope128(qpe, cos, sa, sb, half) * scale).astype(q_ref.dtype)
        k_ref[:, nope] = _rms128(kv[:, h * LANES:(h + 1) * LANES], knn, LANES).astype(k_ref.dtype)
        k_ref[:, pe] = kpe
    v_ref[...] = kv[:, MIX_W:].astype(v_ref.dtype)


def _mla_prep(p1, g_qa, g_kva, wq, wkv, l, qnn, qnp, knn, knp, tables, *, B, T, scale):
    nt = T // ROW_TILE
    const = lambda b, t: (0, 0)
    gain_spec = pl.BlockSpec((1, LANES), const)
    tab_spec = pl.BlockSpec((ROW_TILE, LANES), lambda b, t: (t, 0))
    out_qk = jax.ShapeDtypeStruct((B, T, N_HEADS * MLA_QK), jnp.bfloat16)
    spec_qk = pl.BlockSpec((None, ROW_TILE, N_HEADS * MLA_QK), lambda b, t: (b, t, 0))
    return pl.pallas_call(
        functools.partial(_mla_prep_kernel, scale=scale),
        out_shape=(out_qk, out_qk, jax.ShapeDtypeStruct((B, T, MIX_W), jnp.bfloat16)),
        grid=(B, nt),
        in_specs=[
            pl.BlockSpec((ROW_TILE, MLA_IN_W), lambda b, t: (b * nt + t, C1_MLA // MLA_IN_W)),
            pl.BlockSpec((1, MLA_Q_LORA), const),
            pl.BlockSpec((1, MLA_KV_LORA), const),
            pl.BlockSpec((None,) + wq.shape[1:], lambda b, t: (l, 0, 0)),
            pl.BlockSpec((None,) + wkv.shape[1:], lambda b, t: (l, 0, 0)),
            gain_spec, gain_spec, gain_spec, gain_spec, tab_spec, tab_spec, tab_spec,
        ],
        out_specs=(spec_qk, spec_qk, pl.BlockSpec((None, ROW_TILE, MIX_W), lambda b, t: (b, t, 0))),
        compiler_params=_cparams("arbitrary", "arbitrary"),
        name="mla_prep",
    )(p1, g_qa, g_kva, wq, wkv, qnn, qnp, knn, knp, *tables)


def _window_kernel(q_ref, k_ref, v_ref, g_ref, sink_ref, o_ref, *, S, C):
    blk = WINDOW
    band = 3 * blk
    t = pl.program_id(2)
    kc = k_ref[S:S + C, :]
    vc = v_ref[S:S + C, :]

    def stack(x):
        return jnp.concatenate([x[:, r * HEAD_DIM:(r + 1) * HEAD_DIM] for r in range(WA_GROUP)], axis=0)

    def sink_col(n):
        return jnp.concatenate([jnp.broadcast_to(sink_ref[r:r + 1, :1], (n, 1)) for r in range(WA_GROUP)], axis=0)

    def attend(parts, sink):
        m = sink
        for s, _ in parts:
            m = jnp.maximum(m, jnp.max(s, axis=-1, keepdims=True))
        den = jnp.exp(sink - m)
        acc = None
        for s, v in parts:
            e = jnp.exp(s - m)
            den = den + jnp.sum(e, axis=-1, keepdims=True)
            pv = _dot(e.astype(v.dtype), v)
            acc = pv if acc is None else acc + pv
        return acc / den

    def store(r0, n, o):
        for r in range(WA_GROUP):
            cols = slice(r * HEAD_DIM, (r + 1) * HEAD_DIM)
            g = g_ref[r0:r0 + n, cols].astype(jnp.float32)
            o_ref[r0:r0 + n, cols] = (o[r * n:(r + 1) * n] * _silu(g)).astype(o_ref.dtype)

    @pl.when(t < S // ROW_TILE)
    def _():
        sink = sink_col(blk)
        for i in range(ROW_TILE // blk):
            q0 = t * ROW_TILE + i * blk
            q = stack(q_ref[i * blk:(i + 1) * blk, :])
            k0 = pl.multiple_of(jnp.clip(q0 - blk, 0, S - band), blk)
            kb = k_ref[pl.ds(k0, band), :]
            vb = v_ref[pl.ds(k0, band), :]
            s_lat = _dot_nt(q, kb)
            q_pos = q0 + (lax.broadcasted_iota(jnp.int32, s_lat.shape, 0) & (blk - 1))
            k_pos = k0 + lax.broadcasted_iota(jnp.int32, s_lat.shape, 1)
            s_lat = jnp.where(jnp.abs(q_pos - k_pos) <= WINDOW, s_lat, NEG_INF)
            store(i * blk, blk, attend([(s_lat, vb), (_dot_nt(q, kc), vc)], sink))

    @pl.when(t >= S // ROW_TILE)
    def _():
        q = stack(q_ref[...])
        store(0, ROW_TILE, attend([(_dot_nt(q, kc), vc)], sink_col(ROW_TILE)))


def _window_attn(q, k, p1, sink, *, S, C):
    B, T, _ = q.shape
    nt = T // ROW_TILE
    gw = WA_GROUP * HEAD_DIM
    return pl.pallas_call(
        functools.partial(_window_kernel, S=S, C=C),
        out_shape=jax.ShapeDtypeStruct((B, T, MIX_W), jnp.bfloat16),
        grid=(B, WA_KV_HEADS, nt),
        in_specs=[
            pl.BlockSpec((None, ROW_TILE, gw), lambda b, g, t: (b, t, g)),
            pl.BlockSpec((None, T, HEAD_DIM), lambda b, g, t: (b, 0, g)),
            pl.BlockSpec((T, HEAD_DIM), lambda b, g, t: (b, C1_AV // HEAD_DIM + g)),
            pl.BlockSpec((ROW_TILE, gw), lambda b, g, t: (b * nt + t, C1_AG // gw + g)),
            pl.BlockSpec((None, 8, LANES), lambda b, g, t: (g, 0, 0)),
        ],
        out_specs=pl.BlockSpec((None, ROW_TILE, gw), lambda b, g, t: (b, t, g)),
        compiler_params=_cparams("arbitrary", "arbitrary", "arbitrary"),
        name="window_attn",
    )(q, k, p1, p1, sink)


NB_HEADS = 2


def _neighbour_kernel(q_ref, k_ref, v_ref, g_ref, qg_ref, kg_ref, bias_ref, o_ref, kn_ref, *, S, C, scale):
    rows = S // GRID_W
    n_blk = rows // NA_QROWS
    kn = NA_KROWS * GRID_W
    T = S + C
    t = pl.program_id(2)

    @pl.when(t == 0)
    def _():
        def norm_body(i, carry):
            r0 = pl.multiple_of(i * ROW_TILE, ROW_TILE)
            for hh in range(NB_HEADS):
                cols = slice(hh * HEAD_DIM, (hh + 1) * HEAD_DIM)
                k = _rms128(k_ref[pl.ds(r0, ROW_TILE), cols].astype(jnp.float32), kg_ref[...], LANES)
                kn_ref[pl.ds(r0, ROW_TILE), cols] = k.astype(kn_ref.dtype)
            return carry

        lax.fori_loop(0, T // ROW_TILE, norm_body, 0)

    def softmax_pv(parts):
        m = None
        for s, _ in parts:
            mx = jnp.max(s, axis=-1, keepdims=True)
            m = mx if m is None else jnp.maximum(m, mx)
        den, acc = None, None
        for s, v in parts:
            e = jnp.exp(s - m)
            sm = jnp.sum(e, axis=-1, keepdims=True)
            pv = _dot(e.astype(v.dtype), v)
            den = sm if den is None else den + sm
            acc = pv if acc is None else acc + pv
        return acc / den

    def head(hh, latent):
        cols = slice(hh * HEAD_DIM, (hh + 1) * HEAD_DIM)
        q = (_rms128(q_ref[:, cols].astype(jnp.float32), qg_ref[...], LANES) * scale).astype(kn_ref.dtype)
        parts = [(_dot_nt(q, kn_ref[S:S + C, cols]), v_ref[S:S + C, cols])]
        if latent:
            ws = jnp.clip(t * NA_QROWS - NA_KH // 2, 0, rows - NA_KROWS)
            k0 = pl.multiple_of(ws * GRID_W, GRID_W)
            variant = jnp.where(t == 0, 0, jnp.where(t == n_blk - 1, 2, 1))
            s_lat = _dot_nt(q, kn_ref[pl.ds(k0, kn), cols]) + bias_ref[variant, hh]
            parts.insert(0, (s_lat, v_ref[pl.ds(k0, kn), cols]))
        g = g_ref[:, cols].astype(jnp.float32)
        o_ref[:, cols] = (softmax_pv(parts) * _silu(g)).astype(o_ref.dtype)

    @pl.when(t < n_blk)
    def _():
        for hh in range(NB_HEADS):
            head(hh, True)

    @pl.when(t >= n_blk)
    def _():
        for hh in range(NB_HEADS):
            head(hh, False)


def _neighbour_attn(p1, q_gain, k_gain, bias, l, *, B, S, C, scale):
    T = S + C
    nt = T // ROW_TILE
    bw = NB_HEADS * HEAD_DIM
    assert NA_QROWS * GRID_W == ROW_TILE
    gain_spec = pl.BlockSpec((1, LANES), lambda b, h, t: (0, 0))
    tile = lambda c0: pl.BlockSpec((ROW_TILE, bw), lambda b, h, t: (b * nt + t, c0 // bw + h))
    full = lambda c0: pl.BlockSpec((T, bw), lambda b, h, t: (b, c0 // bw + h))
    return pl.pallas_call(
        functools.partial(_neighbour_kernel, S=S, C=C, scale=scale),
        out_shape=jax.ShapeDtypeStruct((B, T, MIX_W), jnp.bfloat16),
        grid=(B, N_HEADS // NB_HEADS, nt),
        in_specs=[
            tile(C1_BQ), full(C1_BK), full(C1_BV), tile(C1_BG), gain_spec, gain_spec,
            pl.BlockSpec((None, 3, NB_HEADS) + bias.shape[3:], lambda b, h, t: (l, 0, h, 0, 0)),
        ],
        out_specs=pl.BlockSpec((None, ROW_TILE, bw), lambda b, h, t: (b, t, h)),
        scratch_shapes=[pltpu.VMEM((T, bw), jnp.bfloat16)],
        compiler_params=_cparams("arbitrary", "arbitrary", "arbitrary"),
        name="neighbour_attn",
    )(p1, p1, p1, p1, q_gain, k_gain, bias)


def _neighbour_bias(rpb, rows):
    W = GRID_W
    a = np.arange(NA_QROWS)
    j = np.arange(NA_KROWS)
    w = np.arange(W)
    cs = np.clip(w - NA_KW // 2, 0, W - NA_KW)
    col_ok = (w[None, :] >= cs[:, None]) & (w[None, :] < cs[:, None] + NA_KW)
    dc = np.clip(w[None, :] - w[:, None], -(NA_KW - 1), NA_KW - 1) + (NA_KW - 1)
    pick_dc = (dc[None] == np.arange(2 * NA_KW - 1)[:, None, None]).astype(np.float32)
    dr_all, ok_all = [], []
    for r0 in (0, NA_QROWS, rows - NA_QROWS):
        ws = int(np.clip(r0 - NA_KH // 2, 0, rows - NA_KROWS))
        rq = r0 + a
        kr = ws + j
        rs = np.clip(rq - NA_KH // 2, 0, rows - NA_KH)
        row_ok = (kr[None, :] >= rs[:, None]) & (kr[None, :] < rs[:, None] + NA_KH)
        dr_all.append(np.clip(kr[None, :] - rq[:, None] + (NA_KH - 1), 0, 2 * NA_KH - 2))
        ok_all.append(row_ok[:, None, :, None] & col_ok[None, :, None, :])
    dr = np.stack(dr_all)
    ok = np.stack(ok_all)
    rows_sel = rpb.astype(jnp.float32)[:, :, dr, :]
    val = jnp.einsum("lhvajd,dqk->lvhaqjk", rows_sel, pick_dc, precision=lax.Precision.HIGHEST)
    val = jnp.where(ok[None, :, None], val, NEG_INF)
    L, H = rpb.shape[:2]
    return val.reshape(L, 3, H, NA_QROWS * W, NA_KROWS * W)


def _flash_tile(q, k_ref, v_ref, s_ref, acc_ref, *, S, C, chunk, with_latent):
    rows = q.shape[0]
    acc = acc_ref.at[:rows]

    def scores(k0, n):
        return _dot_nt(q, k_ref[pl.ds(k0, n), :])

    def update(s, k0, n, m):
        vals = v_ref[pl.ds(k0, n), :]
        vals = jnp.concatenate([vals, jnp.ones_like(vals)], axis=1)
        mx = jnp.max(s, axis=-1, keepdims=True)
        if m is None:
            m_new = mx
            acc[...] = _dot(jnp.exp2(s - m_new).astype(vals.dtype), vals)
        else:
            m_new = jnp.maximum(m, mx)
            acc[...] = jnp.exp2(m - m_new) * acc[...] + _dot(jnp.exp2(s - m_new).astype(vals.dtype), vals)
        return m_new

    if with_latent:
        n_pairs = S // (2 * chunk)
        s_even, s_odd = s_ref.at[0, :rows], s_ref.at[1, :rows]
        s_even[...] = scores(0, chunk)
        m = update(scores(S, C), S, C, None)

        def pair(k0, m, last):
            s_odd[...] = scores(k0 + chunk, chunk)
            m = update(s_even[...], k0, chunk, m)
            if not last:
                s_even[...] = scores(k0 + 2 * chunk, chunk)
            return update(s_odd[...], k0 + chunk, chunk, m)

        m = lax.fori_loop(0, n_pairs - 1,
                          lambda i, m: pair(pl.multiple_of(i * 2 * chunk, 2 * chunk), m, False), m)
        pair((n_pairs - 1) * 2 * chunk, m, True)
    else:
        update(scores(S, C), S, C, None)
    return acc[:, :HEAD_DIM] / acc[:, HEAD_DIM:]


def _mla_kernel(q_ref, k_ref, v_ref, g_ref, o_ref, s_ref, acc_ref, *, S, C, tq, chunk):
    def tile(r0, n, with_latent):
        o = _flash_tile(q_ref[pl.ds(r0, n), :], k_ref, v_ref, s_ref, acc_ref,
                        S=S, C=C, chunk=chunk, with_latent=with_latent)
        g = g_ref[pl.ds(r0, n), :].astype(jnp.float32)
        o_ref[pl.ds(r0, n), :] = (o * _silu(g)).astype(o_ref.dtype)

    def body(i, carry):
        tile(pl.multiple_of(i * tq, tq), tq, True)
        return carry

    lax.fori_loop(0, S // tq, body, 0)
    tile(S, C, False)


def _flash_scratch(rows, chunk):
    return [pltpu.VMEM((2, rows, chunk), jnp.float32), pltpu.VMEM((rows, 2 * HEAD_DIM), jnp.float32)]


def _mla_attn(q, k, v, p2, *, S, C):
    B, T, _ = q.shape
    head = lambda b, h: (b, 0, h)
    tq, chunk = 1024, min(1024, S // 2)
    return pl.pallas_call(
        functools.partial(_mla_kernel, S=S, C=C, tq=tq, chunk=chunk),
        scratch_shapes=_flash_scratch(tq, chunk),
        out_shape=jax.ShapeDtypeStruct((B, T, MIX_W), jnp.bfloat16),
        grid=(B, N_HEADS),
        in_specs=[
            pl.BlockSpec((None, T, MLA_QK), head),
            pl.BlockSpec((None, T, MLA_QK), head),
            pl.BlockSpec((None, T, HEAD_DIM), head),
            pl.BlockSpec((T, HEAD_DIM), lambda b, h: (b, C2_CG // HEAD_DIM + h)),
        ],
        out_specs=pl.BlockSpec((None, T, HEAD_DIM), head),
        compiler_params=_cparams("arbitrary", "arbitrary"),
        name="mla_attn",
    )(q, k, v, p2)


def _diff_kernel(q_ref, k_ref, v_ref, g_ref, lam_ref, sub_ref, o_ref, s_ref, acc_ref,
                 *, S, C, tq, chunk, lambda_init):
    lv = lam_ref[...]
    lam = (jnp.exp(jnp.sum(lv[0:1] * lv[1:2], axis=-1, keepdims=True))
           - jnp.exp(jnp.sum(lv[2:3] * lv[3:4], axis=-1, keepdims=True)) + lambda_init)
    sub = sub_ref[...] * (1.0 - lambda_init)

    def tile(r0, n, with_latent):
        q = q_ref[pl.ds(r0, n), :]
        low = lax.broadcasted_iota(jnp.int32, q.shape, 1) < DIFF_DIM
        zero = jnp.zeros_like(q)
        q2 = jnp.concatenate([jnp.where(low, q, zero), jnp.where(low, zero, q)], axis=0)
        o = _flash_tile(q2, k_ref, v_ref, s_ref, acc_ref, S=S, C=C, chunk=chunk, with_latent=with_latent)
        d = o[:n] - lam * o[n:]
        ms = jnp.mean(d * d, axis=-1, keepdims=True)
        g = g_ref[pl.ds(r0, n), :].astype(jnp.float32)
        o_ref[pl.ds(r0, n), :] = (d * lax.rsqrt(ms + EPS) * sub * _silu(g)).astype(o_ref.dtype)

    def body(i, carry):
        tile(pl.multiple_of(i * tq, tq), tq, True)
        return carry

    lax.fori_loop(0, S // tq, body, 0)
    tile(S, C, False)


def _diff_attn(q, k, p2, lam_rows, subln, *, S, C, lambda_init):
    B, T, _ = q.shape
    head = lambda b, h: (b, 0, h)
    const = lambda b, h: (0, 0)
    tq, chunk = 512, min(1024, S // 2)
    return pl.pallas_call(
        functools.partial(_diff_kernel, S=S, C=C, tq=tq, chunk=chunk, lambda_init=lambda_init),
        scratch_shapes=_flash_scratch(2 * tq, chunk),
        out_shape=jax.ShapeDtypeStruct((B, T, MIX_W), jnp.bfloat16),
        grid=(B, N_HEADS),
        in_specs=[
            pl.BlockSpec((None, T, HEAD_DIM), head),
            pl.BlockSpec((None, T, HEAD_DIM), head),
            pl.BlockSpec((T, HEAD_DIM), lambda b, h: (b, C2_DV // HEAD_DIM + h)),
            pl.BlockSpec((T, HEAD_DIM), lambda b, h: (b, C2_DG // HEAD_DIM + h)),
            pl.BlockSpec((8, LANES), const),
            pl.BlockSpec((1, LANES), const),
        ],
        out_specs=pl.BlockSpec((None, T, HEAD_DIM), head),
        compiler_params=_cparams("arbitrary", "arbitrary"),
        name="diff_attn",
    )(q, k, p2, p2, lam_rows, subln)


def _merge_kernel(ua_ref, ub_ref, uc_ref, ud_ref, m0_ref, m1_ref, m2_ref, m3_ref, w_ref, o_ref):
    acc = None
    for i, (u, mg) in enumerate(zip((ua_ref, ub_ref, uc_ref, ud_ref), (m0_ref, m1_ref, m2_ref, m3_ref))):
        t = _sigmoid(mg[...].astype(jnp.float32)) * _dot(u[...], w_ref[i])
        acc = t if acc is None else acc + t
    o_ref[...] = acc.astype(o_ref.dtype)


def _merge(us, p2, wb, l):
    R = p2.shape[0]
    D = wb.shape[-1]
    tm, tn = _row_tile(R, 640), 1024
    u_spec = pl.BlockSpec((tm, MIX_W), lambda m, n: (m, 0))
    mg_specs = [pl.BlockSpec((tm, tn), functools.partial(lambda m, n, i: (m, (C2_MG + i * D) // tn + n), i=i))
                for i in range(4)]
    return pl.pallas_call(
        _merge_kernel,
        out_shape=jax.ShapeDtypeStruct((R, D), jnp.bfloat16),
        grid=(R // tm, D // tn),
        in_specs=[u_spec, u_spec, u_spec, u_spec, *mg_specs,
                  pl.BlockSpec((None, 4, MIX_W, tn), lambda m, n: (l, 0, 0, n))],
        out_specs=pl.BlockSpec((tm, tn), lambda m, n: (m, n)),
        compiler_params=_cparams("arbitrary", "arbitrary"),
        name="merge",
    )(*[u.reshape(R, MIX_W) for u in us], p2, p2, p2, p2, wb)


def _out_kernel(a_ref, w_ref, x_ref, gl_ref, gc_ref, o_ref, *, S):
    tm = o_ref.shape[0]
    row = pl.program_id(2) * tm + lax.broadcasted_iota(jnp.int32, o_ref.shape, 0)
    gate = jnp.where(row < S, gl_ref[...], gc_ref[...])
    o_ref[...] = x_ref[...] + gate * _dot(a_ref[...], w_ref[...])


def _out_proj(merged, w, xs, mods4, l, S, out_rows):
    B, T, D = xs.shape
    tm, tn = _row_tile(out_rows, 640), 1024
    gate_col = lambda n: 2 * (D // tn) + n
    return pl.pallas_call(
        functools.partial(_out_kernel, S=S),
        out_shape=jax.ShapeDtypeStruct((B, out_rows, D), jnp.float32),
        grid=(D // tn, B, out_rows // tm),
        in_specs=[
            pl.BlockSpec((None, tm, D), lambda n, b, t: (b, t, 0)),
            pl.BlockSpec((None, D, tn), lambda n, b, t: (l, 0, n)),
            pl.BlockSpec((None, tm, tn), lambda n, b, t: (b, t, n)),
            pl.BlockSpec((None, None, 1, tn), lambda n, b, t: (l, b, 0, gate_col(n))),
            pl.BlockSpec((None, None, 1, tn), lambda n, b, t: (l, B, 0, gate_col(n))),
        ],
        out_specs=pl.BlockSpec((None, tm, tn), lambda n, b, t: (b, t, n)),
        compiler_params=_cparams("arbitrary", "arbitrary", "arbitrary"),
        name="out_proj",
    )(merged.reshape(B, T, D), w, xs, mods4, mods4)


def _rope_tables(S, C, d_rot):
    t = jnp.arange(S, dtype=jnp.int32)
    row = (t // GRID_W).astype(jnp.float32)
    col = (t % GRID_W).astype(jnp.float32)
    n_f = d_rot // 4
    inv = jnp.power(ROPE_BASE, -jnp.arange(n_f, dtype=jnp.float32) / n_f)
    ang = jnp.concatenate([row[:, None] * inv, col[:, None] * inv], axis=-1)
    cos, sin = jnp.cos(ang), jnp.sin(ang)
    zero = jnp.zeros_like(sin)
    reps = LANES // d_rot
    cos_t = jnp.tile(jnp.concatenate([cos, cos], axis=-1), (1, reps))
    sa_t = jnp.tile(jnp.concatenate([-sin, zero], axis=-1), (1, reps))
    sb_t = jnp.tile(jnp.concatenate([zero, sin], axis=-1), (1, reps))
    pad = lambda a, v: jnp.concatenate([a, jnp.full((C, LANES), v, jnp.float32)], axis=0)
    return pad(cos_t, 1.0), pad(sa_t, 0.0), pad(sb_t, 0.0)


def _split_w_q_up(w):
    L, K, _ = w.shape
    w = w.reshape(L, K, N_HEADS, MLA_NOPE + MLA_ROPE)
    w = jnp.pad(w, ((0, 0), (0, 0), (0, 0), (0, MLA_QK - MLA_NOPE - MLA_ROPE)))
    return w.reshape(L, K, N_HEADS * MLA_QK).astype(jnp.bfloat16)


def _split_w_kv_up(w):
    L, K, _ = w.shape
    w = w.reshape(L, K, N_HEADS, 2, HEAD_DIM)
    return jnp.swapaxes(w, 2, 3).reshape(L, K, 2 * MIX_W).astype(jnp.bfloat16)


def _lane_row(v, reps=1, pad=0):
    r = jnp.tile(v.astype(jnp.float32), reps)
    return jnp.pad(r, (0, pad)).reshape(1, LANES)


def kernel(x, c, ctx, c_ctx, w_ada, b_ada, g_norm, w_in, qn_a, kn_a, sink_a, qn_b, kn_b, rpb_b, g_qa, g_kva,
           w_q_up, w_kv_up, qn_nope, qn_pe, kn_nope, kn_pe, qn_d, kn_d, lam_q1, lam_k1, lam_q2, lam_k2,
           subln_d, w_branch, w_out):
    B, S, D = x.shape
    C = ctx.shape[1]
    L = w_ada.shape[0]
    T = S + C
    R = B * T
    rows = S // GRID_W
    assert D == 4 * MIX_W and S % 1024 == 0 and C == ROW_TILE and B + 1 <= 8
    assert rows % NA_QROWS == 0 and rows >= NA_KROWS + NA_QROWS
    assert w_in.shape[-1] == P2_START + C2_MG + 4 * D

    xs = jnp.concatenate([x, ctx], axis=1)
    cc = jnp.zeros((8, D), jnp.float32).at[:B].set(c).at[B].set(c_ctx)
    mods4 = _ada(cc, w_ada, b_ada).reshape(L, 8, 1, 3 * D)

    w_in_t = jnp.swapaxes(w_in, 1, 2)
    w1_t = w_in_t[:, :P1_W].astype(jnp.bfloat16)
    w2_t = w_in_t[:, P2_START:].astype(jnp.bfloat16)
    w_q_p = _split_w_q_up(w_q_up)
    w_kv_p = _split_w_kv_up(w_kv_up)
    w_branch_p = w_branch.astype(jnp.bfloat16)
    w_out_p = w_out.astype(jnp.bfloat16)
    tab_a = _rope_tables(S, C, HEAD_DIM)
    tab_d = _rope_tables(S, C, DIFF_DIM)
    bias_b = _neighbour_bias(rpb_b, rows)

    for l in range(L):
        lambda_init = 0.8 - 0.6 * math.exp(-0.3 * l)
        h = _prologue(xs, g_norm[l], mods4, l, S).reshape(R, D)
        p1 = _in_proj(h, w1_t, l, "in_proj_1")
        p2 = _in_proj(h, w2_t, l, "in_proj_2")

        qa, ka = _normrope(p1, C1_AQ, MIX_W, C1_AK, WA_KV_HEADS * HEAD_DIM, _lane_row(qn_a[l]), _lane_row(kn_a[l]),
                           tab_a, B=B, T=T, group=HEAD_DIM, half=HEAD_DIM // 2, scale=HEAD_DIM ** -0.5,
                           name="prep_window")
        sink = jnp.zeros((WA_KV_HEADS, 8, LANES), jnp.float32).at[:, :WA_GROUP].set(
            jnp.broadcast_to(sink_a[l].astype(jnp.float32).reshape(WA_KV_HEADS, WA_GROUP, 1),
                             (WA_KV_HEADS, WA_GROUP, LANES)))
        ua = _window_attn(qa, ka, p1, sink, S=S, C=C)

        ub = _neighbour_attn(p1, _lane_row(qn_b[l]), _lane_row(kn_b[l]), bias_b, l,
                             B=B, S=S, C=C, scale=HEAD_DIM ** -0.5)

        qc, kc, vc = _mla_prep(p1, g_qa[l].reshape(1, -1), g_kva[l].reshape(1, -1), w_q_p, w_kv_p, l,
                               _lane_row(qn_nope[l]), _lane_row(qn_pe[l], pad=LANES - MLA_ROPE),
                               _lane_row(kn_nope[l]), _lane_row(kn_pe[l], pad=LANES - MLA_ROPE),
                               tab_d, B=B, T=T, scale=(MLA_NOPE + MLA_ROPE) ** -0.5 * LOG2E)
        uc = _mla_attn(qc, kc, vc, p2, S=S, C=C)

        qd, kd = _normrope(p2, C2_DQ, MIX_W, C2_DK, MIX_W, _lane_row(qn_d[l], reps=2), _lane_row(kn_d[l], reps=2),
                           tab_d, B=B, T=T, group=DIFF_DIM, half=DIFF_DIM // 2, scale=DIFF_DIM ** -0.5 * LOG2E,
                           name="prep_diff")
        lam_rows = jnp.zeros((8, LANES), jnp.float32).at[:4, :DIFF_DIM].set(
            jnp.stack([lam_q1[l], lam_k1[l], lam_q2[l], lam_k2[l]]).astype(jnp.float32))
        ud = _diff_attn(qd, kd, p2, lam_rows, _lane_row(subln_d[l]), S=S, C=C, lambda_init=lambda_init)

        merged = _merge((ua, ub, uc, ud), p2, w_branch_p, l)
        xs = _out_proj(merged, w_out_p, xs, mods4, l, S, S if l == L - 1 else T)
    return xs
```
